```python
import math
import jax, jax.numpy as jnp
from jax import lax
import numpy as np

D_MODEL = 1024
BATCH = 4
SEQ = 8192
DEPTH = 4

HEAD_DIM = 64
ATTN_SCALE = HEAD_DIM ** -0.5
Q_BLOCK = 128
A_HEADS = 8
A_LATENT = 128
IDX_HEADS = 4
IDX_DIM = 64
IDX_TOPK_MAX = 256
B_HEADS = 8
MOBA_BLOCK = 256
MOBA_TOPK = 3
C_HEADS = 16
EVEN_SPLITS = (A_HEADS * HEAD_DIM, A_LATENT, IDX_HEADS * IDX_DIM, IDX_DIM, IDX_HEADS, B_HEADS * HEAD_DIM, B_HEADS * HEAD_DIM, B_HEADS * HEAD_DIM)
EVEN_IN = sum(EVEN_SPLITS)
ODD_SPLITS = (C_HEADS * HEAD_DIM, C_HEADS * HEAD_DIM, C_HEADS * HEAD_DIM, C_HEADS)
ODD_IN = sum(ODD_SPLITS)
NUM_BUCKETS = 32
MAX_EXACT = NUM_BUCKETS // 2
MAX_DISTANCE = 1024
N_EXPERTS = 32
TOP_K = 4
D_FF = 1024
SWIGLU_LIMIT = 7.0
SWIGLU_ALPHA = 1.702
ROW_BLOCK = 128
NORM_EPS = 1e-5
DEEPNORM_ALPHA = (2 * DEPTH) ** 0.25
DEEPNORM_BETA = (8 * DEPTH) ** -0.25
N_EVEN = (DEPTH + 1) // 2
N_ODD = DEPTH // 2

kernel_name = 'hybrid_dsa_moba_fox_moe_deepnorm'


def split_cols(h, sizes):
    return jnp.split(h, np.cumsum(sizes)[:-1].tolist(), axis=-1)


def to_heads(t, n):
    return t.reshape(t.shape[0], t.shape[1], n, -1)


def layer_norm(x, g, b):
    xf = x.astype(jnp.float32)
    mu = jnp.mean(xf, -1, keepdims=True)
    var = jnp.mean(jnp.square(xf - mu), -1, keepdims=True)
    y = (xf - mu) * lax.rsqrt(var + NORM_EPS)
    return (y * g.astype(jnp.float32) + b.astype(jnp.float32)).astype(x.dtype)


def rms_norm(x, g):
    xf = x.astype(jnp.float32)
    y = xf * lax.rsqrt(jnp.mean(jnp.square(xf), -1, keepdims=True) + NORM_EPS)
    return (y * g.astype(jnp.float32)).astype(x.dtype)


def rel_bucket(dist):
    n = jnp.maximum(dist, 0)
    nf = jnp.maximum(n, MAX_EXACT).astype(jnp.float32)
    log_b = MAX_EXACT + (jnp.log(nf / MAX_EXACT) / math.log(MAX_DISTANCE / MAX_EXACT) * (NUM_BUCKETS - MAX_EXACT)).astype(jnp.int32)
    return jnp.where(n < MAX_EXACT, n, jnp.minimum(log_b, NUM_BUCKETS - 1))


def query_blocks(fn, n_batch, T):
    out = lax.map(fn, jnp.arange(T // Q_BLOCK))
    return jnp.moveaxis(out, 0, 1).reshape(n_batch, T, -1)


def dsa_attention(q, c_kv, q_idx, k_idx, w_idx, w_uk, w_uv, bias_tab):
    n_batch, T = q.shape[0], q.shape[1]
    topk = min(IDX_TOPK_MAX, T // 4)
    q_lat = jnp.einsum('bthd,hdc->bthc', q, w_uk)
    pos = jnp.arange(T)
    b_ix = jnp.arange(n_batch)[:, None, None]

    def block(i):
        s0 = i * Q_BLOCK
        tq = s0 + jnp.arange(Q_BLOCK)
        ql = lax.dynamic_slice_in_dim(q_lat, s0, Q_BLOCK, axis=1)
        qi = lax.dynamic_slice_in_dim(q_idx, s0, Q_BLOCK, axis=1)
        wi = lax.dynamic_slice_in_dim(w_idx, s0, Q_BLOCK, axis=1)
        rel = jax.nn.relu(jnp.einsum('bqjd,bsd->bqjs', qi, k_idx))
        score = jnp.einsum('bqj,bqjs->bqs', wi, rel).astype(jnp.float32)
        score = jnp.where(pos[None, None, :] <= tq[None, :, None], score, -jnp.inf)
        _, sel = lax.top_k(score, topk)
        c_sel = c_kv[b_ix, sel]
        logits = jnp.einsum('bqhc,bqkc->bqhk', ql, c_sel).astype(jnp.float32) * ATTN_SCALE
        dist = tq[None, :, None] - sel
        bias = jnp.swapaxes(bias_tab[rel_bucket(dist)], -1, -2).astype(jnp.float32)
        logits = jnp.where((dist >= 0)[:, :, None, :], logits + bias, -jnp.inf)
        p = jax.nn.softmax(logits, axis=-1).astype(c_sel.dtype)
        o_lat = jnp.einsum('bqhk,bqkc->bqhc', p, c_sel)
        return jnp.einsum('bqhc,hcd->bqhd', o_lat, w_uv)

    return query_blocks(block, n_batch, T)


def moba_attention(q, k, v, bias_tab):
    n_batch, T, H, dh = q.shape
    nb = -(-T // MOBA_BLOCK)
    pad = nb * MOBA_BLOCK - T
    kp = jnp.pad(k, ((0, 0), (0, pad), (0, 0), (0, 0)))
    vp = jnp.pad(v, ((0, 0), (0, pad), (0, 0), (0, 0)))
    kbh = kp.reshape(n_batch, nb, MOBA_BLOCK, H, dh).transpose(0, 3, 1, 2, 4)
    vbh = vp.reshape(n_batch, nb, MOBA_BLOCK, H, dh).transpose(0, 3, 1, 2, 4)
    kbar = jnp.mean(kbh.astype(jnp.float32), axis=3).astype(k.dtype)
    ksel = min(MOBA_TOPK, nb)
    b_ix = jnp.arange(n_batch)[:, None, None]
    h_ix = jnp.arange(H)[None, None, :]
    offs = jnp.arange(MOBA_BLOCK)

    def block(i):
        s0 = i * Q_BLOCK
        tq = s0 + jnp.arange(Q_BLOCK)
        own = s0 // MOBA_BLOCK
        qc = lax.dynamic_slice_in_dim(q, s0, Q_BLOCK, axis=1)
        gate = jnp.einsum('bqhd,bhnd->bqhn', qc, kbar).astype(jnp.float32)
        gate = jnp.where(jnp.arange(nb) < own, gate, -jnp.inf)
        _, sel = lax.top_k(gate, ksel)
        blks = [sel[..., j] for j in range(ksel)]
        parts = []
        for j, blk in enumerate(blks):
            kj = kbh[b_ix, h_ix, blk]
            sj = jnp.einsum('bqhd,bqhkd->bqhk', qc, kj).astype(jnp.float32) * ATTN_SCALE
            bucket = rel_bucket(tq[None, :, None, None] - (blk[..., None] * MOBA_BLOCK + offs))
            sj = sj + bias_tab[bucket, h_ix[..., None]].astype(jnp.float32)
            parts.append(jnp.where(j < own, sj, -jnp.inf))
        start = own * MOBA_BLOCK
        ko = lax.dynamic_slice_in_dim(kp, start, MOBA_BLOCK, axis=1)
        vo = lax.dynamic_slice_in_dim(vp, start, MOBA_BLOCK, axis=1)
        dist_o = tq[:, None] - (start + offs)[None, :]
        so = jnp.einsum('bqhd,bkhd->bqhk', qc, ko).astype(jnp.float32) * ATTN_SCALE
        so = so + jnp.swapaxes(bias_tab[rel_bucket(dist_o)], -1, -2).astype(jnp.float32)
        parts.append(jnp.where((dist_o >= 0)[:, None, :], so, -jnp.inf))
        p = jax.nn.softmax(jnp.concatenate(parts, axis=-1), axis=-1).astype(v.dtype)
        out = jnp.einsum('bqhk,bkhd->bqhd', p[..., ksel * MOBA_BLOCK:], vo)
        for j, blk in enumerate(blks):
            pj = p[..., j * MOBA_BLOCK:(j + 1) * MOBA_BLOCK]
            out = out + jnp.einsum('bqhk,bqhkd->bqhd', pj, vbh[b_ix, h_ix, blk])
        return out

    return query_blocks(block, n_batch, T)


def fox_attention(q, k, v, f_logit):
    n_batch, T = q.shape[0], q.shape[1]
    cum = jnp.cumsum(jax.nn.log_sigmoid(f_logit.astype(jnp.float32)), axis=1)
    cum_s = jnp.swapaxes(cum, 1, 2)[:, None]
    pos = jnp.arange(T)

    def block(i):
        s0 = i * Q_BLOCK
        tq = s0 + jnp.arange(Q_BLOCK)
        qc = lax.dynamic_slice_in_dim(q, s0, Q_BLOCK, axis=1)
        cq = lax.dynamic_slice_in_dim(cum, s0, Q_BLOCK, axis=1)
        logits = jnp.einsum('bqhd,bshd->bqhs', qc, k).astype(jnp.float32) * ATTN_SCALE + (cq[..., None] - cum_s)
        logits = jnp.where((pos[None, :] <= tq[:, None])[:, None, :], logits, -jnp.inf)
        p = jax.nn.softmax(logits, axis=-1).astype(v.dtype)
        return jnp.einsum('bqhs,bshd->bqhd', p, v)

    return query_blocks(block, n_batch, T)


def even_mixer(x, w_in, w_uk, w_uv, kv_g, w_out, rel_bias):
    qa, ckv, qi, ki, wi, qb, kb, vb = split_cols(x @ w_in, EVEN_SPLITS)
    oa = dsa_attention(to_heads(qa, A_HEADS), rms_norm(ckv, kv_g), to_heads(qi, IDX_HEADS), ki, wi,
                       w_uk, w_uv, rel_bias[:, :A_HEADS])
    ob = moba_attention(to_heads(qb, B_HEADS), to_heads(kb, B_HEADS), to_heads(vb, B_HEADS),
                        rel_bias[:, A_HEADS:])
    return jnp.concatenate([oa, ob], axis=-1) @ w_out


def odd_mixer(x, w_in, b_f, w_out):
    q, k, v, fz = split_cols(x @ w_in, ODD_SPLITS)
    o = fox_attention(to_heads(q, C_HEADS), to_heads(k, C_HEADS), to_heads(v, C_HEADS), fz + b_f)
    return o @ w_out


def moe(x, router_w, router_b, w_gu, b_gu, w_down, b_down):
    n_batch, T, D = x.shape
    N = n_batch * T
    xf = x.reshape(N, D)
    logits = (xf @ router_w + router_b).astype(jnp.float32)
    top_logit, top_e = lax.top_k(logits, TOP_K)
    gates = jax.nn.softmax(top_logit, axis=-1)
    flat_e = top_e.reshape(-1)
    flat_tok = jnp.repeat(jnp.arange(N), TOP_K)
    flat_g = gates.reshape(-1)
    order = jnp.argsort(flat_e)
    se = flat_e[order]
    counts = jnp.bincount(flat_e, length=N_EXPERTS)
    padded = (counts + ROW_BLOCK - 1) // ROW_BLOCK * ROW_BLOCK
    start = jnp.cumsum(counts) - counts
    pend = jnp.cumsum(padded)
    pstart = pend - padded
    dest = pstart[se] + (jnp.arange(N * TOP_K) - start[se])
    n_blocks = -(-(N * TOP_K) // ROW_BLOCK) + N_EXPERTS
    n_rows = n_blocks * ROW_BLOCK
    row_tok = jnp.zeros((n_rows,), jnp.int32).at[dest].set(flat_tok[order])
    row_gate = jnp.zeros((n_rows,), jnp.float32).at[dest].set(flat_g[order])
    block_e = jnp.minimum(jnp.searchsorted(pend, jnp.arange(n_blocks) * ROW_BLOCK, side='right'), N_EXPERTS - 1)

    def expert_rows(args):
        tok, e = args
        h = xf[tok] @ w_gu[e] + b_gu[e]
        g, u = jnp.split(h, 2, axis=-1)
        g = jnp.minimum(g, SWIGLU_LIMIT)
        u = jnp.clip(u, -SWIGLU_LIMIT, SWIGLU_LIMIT)
        act = (u + 1.0) * g * jax.nn.sigmoid(SWIGLU_ALPHA * g)
        return act @ w_down[e] + b_down[e]

    y_rows = lax.map(expert_rows, (row_tok.reshape(n_blocks, ROW_BLOCK), block_e)).reshape(n_rows, D)
    y_rows = y_rows * row_gate[:, None].astype(y_rows.dtype)
    y = jax.ops.segment_sum(y_rows, row_tok, num_segments=N)
    return y.reshape(n_batch, T, D)


def setup_inputs(seed: int = 0) -> dict:
    key = jax.random.key(seed)
    ks = jax.random.split(key, 18)

    def nrm(k, shape, s):
        return jax.random.normal(k, shape, jnp.float32) * s

    return {
        'x': nrm(ks[0], (BATCH, SEQ, D_MODEL), 1.0),
        'even_w_in': nrm(ks[1], (N_EVEN, D_MODEL, EVEN_IN), D_MODEL ** -0.5),
        'even_w_uk': nrm(ks[2], (N_EVEN, A_HEADS, HEAD_DIM, A_LATENT), A_LATENT ** -0.5),
        'even_w_uv': nrm(ks[3], (N_EVEN, A_HEADS, A_LATENT, HEAD_DIM), A_LATENT ** -0.5),
        'even_kv_g': 1.0 + nrm(ks[4], (N_EVEN, A_LATENT), 0.02),
        'even_w_out': nrm(ks[5], (N_EVEN, D_MODEL, D_MODEL), D_MODEL ** -0.5 * DEEPNORM_BETA),
        'odd_w_in': nrm(ks[6], (N_ODD, D_MODEL, ODD_IN), D_MODEL ** -0.5),
        'odd_b_f': 2.0 + nrm(ks[7], (N_ODD, C_HEADS), 0.5),
        'odd_w_out': nrm(ks[8], (N_ODD, D_MODEL, D_MODEL), D_MODEL ** -0.5 * DEEPNORM_BETA),
        'rel_bias': nrm(ks[9], (NUM_BUCKETS, A_HEADS + B_HEADS), 0.1),
        'ln_g': 1.0 + nrm(ks[10], (DEPTH, 2, D_MODEL), 0.02),
        'ln_b': nrm(ks[11], (DEPTH, 2, D_MODEL), 0.02),
        'router_w': nrm(ks[12], (DEPTH, D_MODEL, N_EXPERTS), D_MODEL ** -0.5),
        'router_b': nrm(ks[13], (DEPTH, N_EXPERTS), 0.01),
        'exp_w_gu': nrm(ks[14], (DEPTH, N_EXPERTS, D_MODEL, 2 * D_FF), D_MODEL ** -0.5),
        'exp_b_gu': nrm(ks[15], (DEPTH, N_EXPERTS, 2 * D_FF), 0.02),
        'exp_w_down': nrm(ks[16], (DEPTH, N_EXPERTS, D_FF, D_MODEL), D_FF ** -0.5 * DEEPNORM_BETA),
        'exp_b_down': nrm(ks[17], (DEPTH, N_EXPERTS, D_MODEL), 0.02),
    }


def reference(x, even_w_in, even_w_uk, even_w_uv, even_kv_g, even_w_out, odd_w_in, odd_b_f, odd_w_out,
              rel_bias, ln_g, ln_b, router_w, router_b, exp_w_gu, exp_b_gu, exp_w_down, exp_b_down):
    for layer in range(DEPTH):
        j = layer // 2
        if layer % 2 == 0:
            m = even_mixer(x, even_w_in[j], even_w_uk[j], even_w_uv[j], even_kv_g[j], even_w_out[j], rel_bias)
        else:
            m = odd_mixer(x, odd_w_in[j], odd_b_f[j], odd_w_out[j])
        x = layer_norm(DEEPNORM_ALPHA * x + m, ln_g[layer, 0], ln_b[layer, 0])
        f = moe(x, router_w[layer], router_b[layer], exp_w_gu[layer], exp_b_gu[layer],
                exp_w_down[layer], exp_b_down[layer])
        x = layer_norm(DEEPNORM_ALPHA * x + f, ln_g[layer, 1], ln_b[layer, 1])
    return x
```

```python
import functools
import math

import numpy as np
import jax
import jax.numpy as jnp
from jax import lax
from jax.experimental import pallas as pl
from jax.experimental.pallas import tpu as pltpu

F32 = jnp.float32
BF16 = jnp.bfloat16
I32 = jnp.int32

HEAD_DIM = 64
ATTN_SCALE = HEAD_DIM ** -0.5
A_HEADS = 8
A_LATENT = 128
IDX_HEADS = 4
IDX_DIM = 64
IDX_TOPK_MAX = 256
B_HEADS = 8
MOBA_BLOCK = 256
MOBA_TOPK = 3
C_HEADS = 16
EVEN_SPLITS = (A_HEADS * HEAD_DIM, A_LATENT, IDX_HEADS * IDX_DIM, IDX_DIM, IDX_HEADS,
               B_HEADS * HEAD_DIM, B_HEADS * HEAD_DIM, B_HEADS * HEAD_DIM)
ODD_SPLITS = (C_HEADS * HEAD_DIM, C_HEADS * HEAD_DIM, C_HEADS * HEAD_DIM, C_HEADS)
NUM_BUCKETS = 32
MAX_EXACT = NUM_BUCKETS // 2
MAX_DISTANCE = 1024
N_EXPERTS = 32
TOP_K = 4
SWIGLU_LIMIT = 7.0
SWIGLU_ALPHA = 1.702
NORM_EPS = 1e-5

LANES = 128
MASKED = -1e30
KEY_NEG_INF = -2139095041
VMEM_LIMIT = 56 * 1024 * 1024

DSA_Q = 128
DSA_CK = 512
DSA_SUB = 128
MOE_ROWS = 256
FOX_T = 512


def _cparams(sem):
    return pltpu.CompilerParams(dimension_semantics=sem, vmem_limit_bytes=VMEM_LIMIT)


def _split_cols(h, sizes):
    offs = np.cumsum((0,) + tuple(sizes))
    return [h[..., int(offs[i]):int(offs[i + 1])] for i in range(len(sizes))]


def _rel_bucket(dist):
    n = jnp.maximum(dist, 0)
    nf = jnp.maximum(n, MAX_EXACT).astype(F32)
    log_b = MAX_EXACT + (jnp.log(nf / MAX_EXACT) / math.log(MAX_DISTANCE / MAX_EXACT)
                         * (NUM_BUCKETS - MAX_EXACT)).astype(I32)
    return jnp.where(n < MAX_EXACT, n, jnp.minimum(log_b, NUM_BUCKETS - 1))


def _toeplitz_tiles(bias_by_dist, n_tiles, edge):
    i = jnp.arange(edge)[:, None]
    j = jnp.arange(edge)[None, :]
    d = jnp.arange(n_tiles)[:, None, None] * edge + (i - j)[None]
    d = jnp.clip(d, 0, bias_by_dist.shape[0] - 1)
    return jnp.moveaxis(bias_by_dist[d], -1, 1)


def _layer_norm(v, g, b):
    mu = jnp.mean(v, axis=-1, keepdims=True)
    d = v - mu
    var = jnp.mean(d * d, axis=-1, keepdims=True)
    return d * lax.rsqrt(var + NORM_EPS) * g + b


def _linear_kernel(x_ref, w_ref, o_ref):
    o_ref[...] = jnp.dot(x_ref[...].astype(BF16), w_ref[...], preferred_element_type=F32)


def _linear(x, w, tm=256):
    m, k = x.shape
    n = w.shape[1]
    return pl.pallas_call(
        _linear_kernel,
        grid=(m // tm,),
        in_specs=[pl.BlockSpec((tm, k), lambda i: (i, 0)),
                  pl.BlockSpec((k, n), lambda i: (0, 0))],
        out_specs=pl.BlockSpec((tm, n), lambda i: (i, 0)),
        out_shape=jax.ShapeDtypeStruct((m, n), F32),
        compiler_params=_cparams(("parallel",)),
        name="linear",
    )(x, w)


def _linear_ln_kernel(a_ref, w_ref, x_ref, g_ref, b_ref, o_ref, ob_ref, *, alpha):
    m = jnp.dot(a_ref[...].astype(BF16), w_ref[...], preferred_element_type=F32)
    y = _layer_norm(alpha * x_ref[...] + m, g_ref[...], b_ref[...])
    o_ref[...] = y
    ob_ref[...] = y.astype(BF16)


def _linear_ln(a, w, x, g, b, alpha, tm=256):
    m, k = a.shape
    n = w.shape[1]
    return pl.pallas_call(
        functools.partial(_linear_ln_kernel, alpha=alpha),
        grid=(m // tm,),
        in_specs=[pl.BlockSpec((tm, k), lambda i: (i, 0)),
                  pl.BlockSpec((k, n), lambda i: (0, 0)),
                  pl.BlockSpec((tm, n), lambda i: (i, 0)),
                  pl.BlockSpec((1, n), lambda i: (0, 0)),
                  pl.BlockSpec((1, n), lambda i: (0, 0))],
        out_specs=[pl.BlockSpec((tm, n), lambda i: (i, 0)),
                   pl.BlockSpec((tm, n), lambda i: (i, 0))],
        out_shape=[jax.ShapeDtypeStruct((m, n), F32), jax.ShapeDtypeStruct((m, n), BF16)],
        compiler_params=_cparams(("parallel",)),
        name="linear_ln",
    )(a, w, x, g.reshape(1, n), b.reshape(1, n))


def _float_key(s):
    bits = lax.bitcast_convert_type(s, I32)
    return bits ^ ((bits >> 31) & 0x7FFFFFFF)


def _dsa_kernel(qa_ref, qi_ref, wi_ref, kidx_ref, ckv_ref, kvg_ref, wuk_ref, wuv_ref, bias_ref,
                o_ref, key_ref, cn_ref, m_ref, l_ref, acc_ref, *, seq, topk, idx_bits):
    qb = pl.program_id(1)
    s0 = qb * DSA_Q
    nc = (s0 + DSA_Q + DSA_CK - 1) // DSA_CK
    n_sub = DSA_CK // LANES
    n_bias = bias_ref.shape[0]

    @pl.when(qb == 0)
    def _():
        def norm_chunk(c, carry):
            rows = pl.ds(pl.multiple_of(c * DSA_CK, DSA_CK), DSA_CK)
            v = ckv_ref[0, rows, :]
            y = v * lax.rsqrt(jnp.mean(v * v, axis=-1, keepdims=True) + NORM_EPS) * kvg_ref[...]
            cn_ref[rows, :] = y.astype(BF16)
            return carry
        lax.fori_loop(0, seq // DSA_CK, norm_chunk, 0)

    t_pos = s0 + lax.broadcasted_iota(I32, (DSA_Q, LANES), 0)
    lane = lax.broadcasted_iota(I32, (DSA_Q, LANES), 1)

    qi = qi_ref[0].reshape(IDX_HEADS * DSA_Q, IDX_DIM)
    wi = wi_ref[0]

    def score_chunk(c, carry):
        rows = pl.ds(pl.multiple_of(c * DSA_CK, DSA_CK), DSA_CK)
        kc = kidx_ref[0, rows, :]
        r = lax.dot_general(qi, kc, (((1,), (1,)), ((), ())), preferred_element_type=F32,
                            precision=lax.Precision.HIGHEST)
        r = jnp.maximum(r, 0.0)
        sc = wi[:, 0:1] * r[0:DSA_Q]
        for j in range(1, IDX_HEADS):
            sc = sc + wi[:, j:j + 1] * r[j * DSA_Q:(j + 1) * DSA_Q]
        sc = jnp.where(sc == 0.0, 0.0, sc)
        for u in range(n_sub):
            s_pos = c * DSA_CK + u * LANES + lane
            su = sc[:, u * LANES:(u + 1) * LANES]
            key_ref[c, :, u * LANES:(u + 1) * LANES] = jnp.where(
                s_pos <= t_pos, _float_key(su), KEY_NEG_INF)
        return carry
    lax.fori_loop(0, nc, score_chunk, 0)

    def count(pred_fn):
        def body(c, acc):
            k = key_ref[c]
            for u in range(n_sub):
                acc = acc + pred_fn(k[:, u * LANES:(u + 1) * LANES], c * DSA_CK + u * LANES + lane)
            return acc
        acc = lax.fori_loop(0, nc, body, jnp.zeros((DSA_Q, LANES), F32))
        return jnp.sum(acc, axis=1, keepdims=True)

    kf = float(topk)
    cnt0 = count(lambda k, sp: jnp.where(k >= 0, 1.0, 0.0))
    cand0 = jnp.where(cnt0 >= kf, 0, -2147483648).astype(I32)

    def thr_step(it, cand):
        test = cand | jnp.left_shift(jnp.int32(1), 30 - it)
        cnt = count(lambda k, sp: jnp.where(k >= test, 1.0, 0.0))
        return jnp.where(cnt >= kf, test, cand)
    thr = lax.fori_loop(0, 31, thr_step, cand0)

    n_gt = count(lambda k, sp: jnp.where(k > thr, 1.0, 0.0))
    need = kf - n_gt

    def cut_step(it, cut):
        test = cut | jnp.left_shift(jnp.int32(1), idx_bits - 1 - it)
        cnt = count(lambda k, sp: jnp.where(k == thr, jnp.where(sp < test, 1.0, 0.0), 0.0))
        return jnp.where(cnt < need, test, cut)
    cut = lax.fori_loop(0, idx_bits, cut_step, jnp.zeros((DSA_Q, 1), I32))

    ql = jnp.dot(qa_ref[0].astype(BF16), wuk_ref[...], preferred_element_type=F32) * ATTN_SCALE
    ql = jnp.concatenate([ql[:, h * A_LATENT:(h + 1) * A_LATENT] for h in range(A_HEADS)],
                         axis=0).astype(BF16)
    m_ref[...] = jnp.full(m_ref.shape, MASKED, F32)
    l_ref[...] = jnp.zeros(l_ref.shape, F32)
    acc_ref[...] = jnp.zeros(acc_ref.shape, F32)

    def attn_chunk(c, carry):
        rows = pl.ds(pl.multiple_of(c * DSA_CK, DSA_CK), DSA_CK)
        cc = cn_ref[rows, :]
        lg = lax.dot_general(ql, cc, (((1,), (1,)), ((), ())), preferred_element_type=F32)
        k = key_ref[c]
        sel_parts, bias_parts = [], []
        for u in range(n_sub):
            s_pos = c * DSA_CK + u * LANES + lane
            ku = k[:, u * LANES:(u + 1) * LANES]
            take = jnp.where(ku > thr, 1.0, jnp.where(ku == thr, jnp.where(s_pos <= cut, 1.0, 0.0), 0.0))
            sel_parts.append(jnp.where(s_pos <= t_pos, take, 0.0))
            tile = jnp.clip(qb - (c * n_sub + u), 0, n_bias - 1)
            bias_parts.append(bias_ref[tile])
        sel = jnp.concatenate(sel_parts, axis=1)
        bias = jnp.concatenate(bias_parts, axis=2)
        lg = lg.reshape(A_HEADS, DSA_Q, DSA_CK) + bias
        lg = jnp.where(sel[None] > 0.5, lg, MASKED).reshape(A_HEADS * DSA_Q, DSA_CK)
        m_prev = m_ref[...]
        m_new = jnp.maximum(m_prev, jnp.max(lg, axis=1, keepdims=True))
        a = jnp.exp(m_prev - m_new)
        p = jnp.exp(lg - m_new)
        l_ref[...] = a * l_ref[...] + jnp.sum(p, axis=1, keepdims=True)
        acc_ref[...] = a * acc_ref[...] + jnp.dot(p.astype(BF16), cc, preferred_element_type=F32)
        m_ref[...] = m_new
        return carry
    lax.fori_loop(0, nc, attn_chunk, 0)

    o_lat = (acc_ref[...] / l_ref[...]).astype(BF16)
    out = jnp.dot(o_lat[0:DSA_Q], wuv_ref[0], preferred_element_type=F32)
    for h in range(1, A_HEADS):
        out = out + jnp.dot(o_lat[h * DSA_Q:(h + 1) * DSA_Q], wuv_ref[h], preferred_element_type=F32)
    o_ref[0] = out


def _dsa(qa, qi4, wi, kidx, ckv, kv_g, w_uk, w_uv, bias_tiles):
    bsz, seq, _ = qa.shape
    topk = min(IDX_TOPK_MAX, seq // 4)
    idx_bits = max(1, int(math.ceil(math.log2(seq))))
    hd = A_HEADS * HEAD_DIM
    wuk_bd = jnp.zeros((hd, A_HEADS * A_LATENT), F32)
    wuv_ex = jnp.zeros((A_HEADS, A_LATENT, hd), F32)
    for h in range(A_HEADS):
        wuk_bd = wuk_bd.at[h * HEAD_DIM:(h + 1) * HEAD_DIM, h * A_LATENT:(h + 1) * A_LATENT].set(w_uk[h])
        wuv_ex = wuv_ex.at[h, :, h * HEAD_DIM:(h + 1) * HEAD_DIM].set(w_uv[h])
    n_bias = bias_tiles.shape[0]
    kern = functools.partial(_dsa_kernel, seq=seq, topk=topk, idx_bits=idx_bits)
    return pl.pallas_call(
        kern,
        grid=(bsz, seq // DSA_Q),
        in_specs=[
            pl.BlockSpec((1, DSA_Q, hd), lambda b, q: (b, q, 0)),
            pl.BlockSpec((1, IDX_HEADS, DSA_Q, IDX_DIM), lambda b, q: (b, 0, q, 0)),
            pl.BlockSpec((1, DSA_Q, IDX_HEADS), lambda b, q: (b, q, 0)),
            pl.BlockSpec((1, seq, IDX_DIM), lambda b, q: (b, 0, 0)),
            pl.BlockSpec((1, seq, A_LATENT), lambda b, q: (b, 0, 0)),
            pl.BlockSpec((1, A_LATENT), lambda b, q: (0, 0)),
            pl.BlockSpec((hd, A_HEADS * A_LATENT), lambda b, q: (0, 0)),
            pl.BlockSpec((A_HEADS, A_LATENT, hd), lambda b, q: (0, 0, 0)),
            pl.BlockSpec((n_bias, A_HEADS, DSA_SUB, DSA_SUB), lambda b, q: (0, 0, 0, 0)),
        ],
        out_specs=pl.BlockSpec((1, DSA_Q, hd), lambda b, q: (b, q, 0)),
        out_shape=jax.ShapeDtypeStruct((bsz, seq, hd), F32),
        scratch_shapes=[
            pltpu.VMEM((seq // DSA_CK, DSA_Q, DSA_CK), I32),
            pltpu.VMEM((seq, A_LATENT), BF16),
            pltpu.VMEM((A_HEADS * DSA_Q, 1), F32),
            pltpu.VMEM((A_HEADS * DSA_Q, 1), F32),
            pltpu.VMEM((A_HEADS * DSA_Q, A_LATENT), F32),
        ],
        compiler_params=_cparams(("arbitrary", "arbitrary")),
        name="dsa",
    )(qa, qi4, wi, kidx, ckv, kv_g.reshape(1, A_LATENT), wuk_bd.astype(BF16), wuv_ex.astype(BF16),
      bias_tiles)


def _kbar_kernel(k_ref, o_ref):
    o_ref[0] = jnp.mean(k_ref[...], axis=0, keepdims=True)


def _kbar(kb, bsz, seq):
    nb = seq // MOBA_BLOCK
    hd = kb.shape[1]
    return pl.pallas_call(
        _kbar_kernel,
        grid=(bsz * nb,),
        in_specs=[pl.BlockSpec((MOBA_BLOCK, hd), lambda i: (i, 0))],
        out_specs=pl.BlockSpec((1, 1, hd), lambda i: (i, 0, 0)),
        out_shape=jax.ShapeDtypeStruct((bsz * nb, 1, hd), F32),
        compiler_params=_cparams(("parallel",)),
        name="kbar",
    )(kb)


def _moba_kernel(q_ref, k_ref, v_ref, kbar_ref, bias_ref, o_ref, sel_ref, m_ref, l_ref, acc_ref, *, nb):
    qi = pl.program_id(2)
    n_bias = bias_ref.shape[1]
    tq = MOBA_BLOCK
    q = q_ref[0, 0]
    qs = q * ATTN_SCALE

    gate = lax.dot_general(q.astype(F32), kbar_ref[0, 0], (((1,), (1,)), ((), ())),
                           preferred_element_type=F32, precision=lax.Precision.HIGHEST)
    blk = lax.broadcasted_iota(I32, (tq, nb), 1)
    g = jnp.where(blk < qi, gate, -jnp.inf)
    rank = jnp.zeros((tq, nb), F32)
    for m in range(nb):
        gm = g[:, m:m + 1]
        rank = rank + jnp.where(gm > g, 1.0, jnp.where(gm == g, jnp.where(blk > m, 1.0, 0.0), 0.0))
    sel = jnp.where(blk < qi, jnp.where(rank < float(MOBA_TOPK), 1.0, 0.0), 0.0)
    for m in range(nb):
        sel_ref[m] = jnp.broadcast_to(sel[:, m:m + 1], (tq, LANES))

    rows = pl.ds(pl.multiple_of(qi * tq, tq), tq)
    kd = k_ref[0, 0, rows, :]
    vd = v_ref[0, 0, rows, :]
    s = lax.dot_general(qs, kd, (((1,), (1,)), ((), ())), preferred_element_type=F32) + bias_ref[0, 0]
    ri = lax.broadcasted_iota(I32, (tq, tq), 0)
    ci = lax.broadcasted_iota(I32, (tq, tq), 1)
    s = jnp.where(ci <= ri, s, MASKED)
    m0 = jnp.max(s, axis=1, keepdims=True)
    p = jnp.exp(s - m0)
    m_ref[...] = m0
    l_ref[...] = jnp.sum(p, axis=1, keepdims=True)
    acc_ref[...] = jnp.dot(p.astype(BF16), vd, preferred_element_type=F32)

    def past_block(n, carry):
        rows_n = pl.ds(pl.multiple_of(n * tq, tq), tq)
        kn = k_ref[0, 0, rows_n, :]
        vn = v_ref[0, 0, rows_n, :]
        sn = lax.dot_general(qs, kn, (((1,), (1,)), ((), ())), preferred_element_type=F32)
        sn = sn + bias_ref[0, jnp.minimum(qi - n, n_bias - 1)]
        sm = sel_ref[n]
        sm = jnp.concatenate([sm] * (tq // LANES), axis=1)
        sn = jnp.where(sm > 0.5, sn, MASKED)
        m_prev = m_ref[...]
        m_new = jnp.maximum(m_prev, jnp.max(sn, axis=1, keepdims=True))
        a = jnp.exp(m_prev - m_new)
        pn = jnp.exp(sn - m_new)
        l_ref[...] = a * l_ref[...] + jnp.sum(pn, axis=1, keepdims=True)
        acc_ref[...] = a * acc_ref[...] + jnp.dot(pn.astype(BF16), vn, preferred_element_type=F32)
        m_ref[...] = m_new
        return carry
    lax.fori_loop(0, qi, past_block, 0)
    o_ref[0, 0] = acc_ref[...] / l_ref[...]


def _moba(q, k, v, kbar, bias_tiles):
    bsz, nh, seq, dh = q.shape
    nb = seq // MOBA_BLOCK
    n_bias = bias_tiles.shape[1]
    return pl.pallas_call(
        functools.partial(_moba_kernel, nb=nb),
        grid=(bsz, nh, nb),
        in_specs=[
            pl.BlockSpec((1, 1, MOBA_BLOCK, dh), lambda b, h, i: (b, h, i, 0)),
            pl.BlockSpec((1, 1, seq, dh), lambda b, h, i: (b, h, 0, 0)),
            pl.BlockSpec((1, 1, seq, dh), lambda b, h, i: (b, h, 0, 0)),
            pl.BlockSpec((1, 1, nb, dh), lambda b, h, i: (b, h, 0, 0)),
            pl.BlockSpec((1, n_bias, MOBA_BLOCK, MOBA_BLOCK), lambda b, h, i: (h, 0, 0, 0)),
        ],
        out_specs=pl.BlockSpec((1, 1, MOBA_BLOCK, dh), lambda b, h, i: (b, h, i, 0)),
        out_shape=jax.ShapeDtypeStruct((bsz, nh, seq, dh), F32),
        scratch_shapes=[
            pltpu.VMEM((nb, MOBA_BLOCK, LANES), F32),
            pltpu.VMEM((MOBA_BLOCK, 1), F32),
            pltpu.VMEM((MOBA_BLOCK, 1), F32),
            pltpu.VMEM((MOBA_BLOCK, dh), F32),
        ],
        compiler_params=_cparams(("parallel", "parallel", "arbitrary")),
        name="moba",
    )(q, k, v, kbar, bias_tiles)


def _cumsum_kernel(f_ref, b_ref, o_ref, *, seq):
    z = f_ref[...] + b_ref[...]
    y = jnp.minimum(z, 0.0) - jnp.log1p(jnp.exp(-jnp.abs(z)))
    lane = lax.broadcasted_iota(I32, y.shape, 1)
    shift = 1
    while shift < seq:
        y = y + jnp.where(lane >= shift, pltpu.roll(y, shift, axis=1), 0.0)
        shift *= 2
    o_ref[...] = y


def _cumsum_logsig(fz_t, bias_rows):
    rows, seq = fz_t.shape
    return pl.pallas_call(
        functools.partial(_cumsum_kernel, seq=seq),
        grid=(1,),
        in_specs=[pl.BlockSpec((rows, seq), lambda i: (0, 0)),
                  pl.BlockSpec((rows, 1), lambda i: (0, 0))],
        out_specs=pl.BlockSpec((rows, seq), lambda i: (0, 0)),
        out_shape=jax.ShapeDtypeStruct((rows, seq), F32),
        compiler_params=_cparams(("arbitrary",)),
        name="cumsum_logsig",
    )(fz_t, bias_rows)


def _fox_kernel(q_ref, k_ref, v_ref, fq_ref, fk_ref, o_ref, m_ref, l_ref, acc_ref):
    qi = pl.program_id(2)
    t = FOX_T
    qs = q_ref[0, 0] * ATTN_SCALE
    fq = fq_ref[0, 0][:, 0:1]

    def logits(n):
        rows = pl.ds(pl.multiple_of(n * t, t), t)
        s = lax.dot_general(qs, k_ref[0, 0, rows, :], (((1,), (1,)), ((), ())),
                            preferred_element_type=F32)
        return s + (fq - fk_ref[0, 0, n]), v_ref[0, 0, rows, :]

    s, vd = logits(qi)
    ri = lax.broadcasted_iota(I32, (t, t), 0)
    ci = lax.broadcasted_iota(I32, (t, t), 1)
    s = jnp.where(ci <= ri, s, MASKED)
    m0 = jnp.max(s, axis=1, keepdims=True)
    p = jnp.exp(s - m0)
    m_ref[...] = m0
    l_ref[...] = jnp.sum(p, axis=1, keepdims=True)
    acc_ref[...] = jnp.dot(p.astype(BF16), vd, preferred_element_type=F32)

    def past_block(n, carry):
        sn, vn = logits(n)
        m_prev = m_ref[...]
        m_new = jnp.maximum(m_prev, jnp.max(sn, axis=1, keepdims=True))
        a = jnp.exp(m_prev - m_new)
        pn = jnp.exp(sn - m_new)
        l_ref[...] = a * l_ref[...] + jnp.sum(pn, axis=1, keepdims=True)
        acc_ref[...] = a * acc_ref[...] + jnp.dot(pn.astype(BF16), vn, preferred_element_type=F32)
        m_ref[...] = m_new
        return carry
    lax.fori_loop(0, qi, past_block, 0)
    o_ref[0, 0] = acc_ref[...] / l_ref[...]


def _fox(q, k, v, cum):
    bsz, nh, seq, dh = q.shape
    t = FOX_T
    nt = seq // t
    fq = jnp.broadcast_to(cum[..., None], (bsz, nh, seq, LANES))
    fk = cum.reshape(bsz, nh, nt, 1, t)
    return pl.pallas_call(
        _fox_kernel,
        grid=(bsz, nh, nt),
        in_specs=[
            pl.BlockSpec((1, 1, t, dh), lambda b, h, i: (b, h, i, 0)),
            pl.BlockSpec((1, 1, seq, dh), lambda b, h, i: (b, h, 0, 0)),
            pl.BlockSpec((1, 1, seq, dh), lambda b, h, i: (b, h, 0, 0)),
            pl.BlockSpec((1, 1, t, LANES), lambda b, h, i: (b, h, i, 0)),
            pl.BlockSpec((1, 1, nt, 1, t), lambda b, h, i: (b, h, 0, 0, 0)),
        ],
        out_specs=pl.BlockSpec((1, 1, t, dh), lambda b, h, i: (b, h, i, 0)),
        out_shape=jax.ShapeDtypeStruct((bsz, nh, seq, dh), F32),
        scratch_shapes=[
            pltpu.VMEM((t, 1), F32),
            pltpu.VMEM((t, 1), F32),
            pltpu.VMEM((t, dh), F32),
        ],
        compiler_params=_cparams(("parallel", "parallel", "arbitrary")),
        name="fox",
    )(q, k, v, fq, fk)


def _router_kernel(x_ref, w_ref, b_ref, e_ref, g_ref):
    lg = jnp.dot(x_ref[...], w_ref[...], preferred_element_type=F32,
                 precision=lax.Precision.HIGHEST) + b_ref[...]
    tm, ne = lg.shape
    col = lax.broadcasted_iota(I32, (tm, ne), 1).astype(F32)
    vals, idxs = [], []
    for _ in range(TOP_K):
        mx = jnp.max(lg, axis=1, keepdims=True)
        ix = jnp.min(jnp.where(lg == mx, col, float(ne)), axis=1, keepdims=True)
        vals.append(mx)
        idxs.append(ix)
        lg = jnp.where(col == ix, -jnp.inf, lg)
    ex = [jnp.exp(v - vals[0]) for v in vals]
    den = ex[0]
    for e in ex[1:]:
        den = den + e
    kcol = lax.broadcasted_iota(I32, (tm, TOP_K), 1)
    e_out = jnp.zeros((tm, TOP_K), I32)
    g_out = jnp.zeros((tm, TOP_K), F32)
    for k in range(TOP_K):
        e_out = jnp.where(kcol == k, idxs[k].astype(I32), e_out)
        g_out = jnp.where(kcol == k, ex[k] / den, g_out)
    e_ref[...] = e_out
    g_ref[...] = g_out


def _router(x, w, b, tm=512):
    n, d = x.shape
    ne = w.shape[1]
    return pl.pallas_call(
        _router_kernel,
        grid=(n // tm,),
        in_specs=[pl.BlockSpec((tm, d), lambda i: (i, 0)),
                  pl.BlockSpec((d, ne), lambda i: (0, 0)),
                  pl.BlockSpec((1, ne), lambda i: (0, 0))],
        out_specs=[pl.BlockSpec((tm, TOP_K), lambda i: (i, 0)),
                   pl.BlockSpec((tm, TOP_K), lambda i: (i, 0))],
        out_shape=[jax.ShapeDtypeStruct((n, TOP_K), I32), jax.ShapeDtypeStruct((n, TOP_K), F32)],
        compiler_params=_cparams(("parallel",)),
        name="router",
    )(x, w, b.reshape(1, ne))


def _expert_kernel(be_ref, nu_ref, xs_ref, wgu_ref, bgu_ref, wd_ref, bd_ref, o_ref):
    i = pl.program_id(0)
    d_ff = wd_ref.shape[1]

    @pl.when(i < nu_ref[0])
    def _():
        h = jnp.dot(xs_ref[...], wgu_ref[0], preferred_element_type=F32) + bgu_ref[0]
        g = jnp.minimum(h[:, :d_ff], SWIGLU_LIMIT)
        u = jnp.clip(h[:, d_ff:], -SWIGLU_LIMIT, SWIGLU_LIMIT)
        act = (u + 1.0) * g * (1.0 / (1.0 + jnp.exp(-SWIGLU_ALPHA * g)))
        o_ref[...] = jnp.dot(act.astype(BF16), wd_ref[0], preferred_element_type=F32) + bd_ref[0]

    @pl.when(i >= nu_ref[0])
    def _():
        o_ref[...] = jnp.zeros(o_ref.shape, o_ref.dtype)


def _experts(xs, block_e, n_used, w_gu, b_gu, w_down, b_down):
    n_rows, d = xs.shape
    ne, _, f2 = w_gu.shape
    d_ff = f2 // 2
    n_blocks = n_rows // MOE_ROWS
    grid_spec = pltpu.PrefetchScalarGridSpec(
        num_scalar_prefetch=2,
        grid=(n_blocks,),
        in_specs=[
            pl.BlockSpec((MOE_ROWS, d), lambda i, be, nu: (i, 0)),
            pl.BlockSpec((1, d, f2), lambda i, be, nu: (be[i], 0, 0)),
            pl.BlockSpec((1, 1, f2), lambda i, be, nu: (be[i], 0, 0)),
            pl.BlockSpec((1, d_ff, d), lambda i, be, nu: (be[i], 0, 0)),
            pl.BlockSpec((1, 1, d), lambda i, be, nu: (be[i], 0, 0)),
        ],
        out_specs=pl.BlockSpec((MOE_ROWS, d), lambda i, be, nu: (i, 0)),
    )
    return pl.pallas_call(
        _expert_kernel,
        grid_spec=grid_spec,
        out_shape=jax.ShapeDtypeStruct((n_rows, d), F32),
        compiler_params=_cparams(("arbitrary",)),
        name="experts",
    )(block_e, n_used, xs, w_gu, b_gu.reshape(ne, 1, f2), w_down, b_down.reshape(ne, 1, d))


def _combine_ln_kernel(y_ref, gt_ref, x_ref, g_ref, b_ref, o_ref, *, alpha):
    gt = gt_ref[...]
    y = y_ref[0] * gt[:, 0:1]
    for k in range(1, TOP_K):
        y = y + y_ref[k] * gt[:, k:k + 1]
    o_ref[...] = _layer_norm(alpha * x_ref[...] + y, g_ref[...], b_ref[...])


def _combine_ln(yg, gates, x, g, b, alpha, tm=256):
    n, d = x.shape
    return pl.pallas_call(
        functools.partial(_combine_ln_kernel, alpha=alpha),
        grid=(n // tm,),
        in_specs=[pl.BlockSpec((TOP_K, tm, d), lambda i: (0, i, 0)),
                  pl.BlockSpec((tm, TOP_K), lambda i: (i, 0)),
                  pl.BlockSpec((tm, d), lambda i: (i, 0)),
                  pl.BlockSpec((1, d), lambda i: (0, 0)),
                  pl.BlockSpec((1, d), lambda i: (0, 0))],
        out_specs=pl.BlockSpec((tm, d), lambda i: (i, 0)),
        out_shape=jax.ShapeDtypeStruct((n, d), F32),
        compiler_params=_cparams(("parallel",)),
        name="combine_ln",
    )(yg, gates, x, g.reshape(1, d), b.reshape(1, d))


def _moe(x, xb, router_w, router_b, w_gu, b_gu, w_down, b_down, ln_g, ln_b, alpha):
    n, d = x.shape
    top_e, gates = _router(x, router_w, router_b)
    nk = n * TOP_K
    flat_e = top_e.reshape(-1)
    order = jnp.argsort(flat_e)
    se = flat_e[order]
    counts = jnp.bincount(flat_e, length=N_EXPERTS)
    padded = (counts + MOE_ROWS - 1) // MOE_ROWS * MOE_ROWS
    start = jnp.cumsum(counts) - counts
    pend = jnp.cumsum(padded)
    pstart = pend - padded
    dest = (pstart[se] + (jnp.arange(nk) - start[se])).astype(I32)
    n_blocks = nk // MOE_ROWS + N_EXPERTS
    n_rows = n_blocks * MOE_ROWS
    row_tok = jnp.zeros((n_rows,), I32).at[dest].set((order // TOP_K).astype(I32))
    pos = jnp.zeros((nk,), I32).at[order].set(dest)
    block_e = jnp.minimum(jnp.searchsorted(pend, jnp.arange(n_blocks) * MOE_ROWS, side='right'),
                          N_EXPERTS - 1).astype(I32)
    n_used = (pend[-1] // MOE_ROWS).astype(I32).reshape(1)
    xs = jnp.take(xb, row_tok, axis=0)
    y_rows = _experts(xs, block_e, n_used, w_gu, b_gu, w_down, b_down)
    pos_t = pos.reshape(n, TOP_K).T.reshape(-1)
    yg = jnp.take(y_rows, pos_t, axis=0).reshape(TOP_K, n, d)
    return _combine_ln(yg, gates, x, ln_g, ln_b, alpha)


def _pad_cols(w, mult=LANES):
    n = w.shape[1]
    pad = (-n) % mult
    return jnp.pad(w, ((0, 0), (0, pad))) if pad else w


def _heads(t, bsz, seq, nh):
    return t.reshape(bsz, seq, nh, HEAD_DIM).transpose(0, 2, 1, 3)


def _even_mixer(xf, bsz, seq, w_in, w_uk, w_uv, kv_g, bias_by_dist):
    proj = _linear(xf, _pad_cols(w_in).astype(BF16))
    qa, ckv, qi, ki, wi, qb, kb, vb = _split_cols(proj[:, :sum(EVEN_SPLITS)], EVEN_SPLITS)
    n_sub_tiles = min(-(-(MAX_DISTANCE) // DSA_SUB) + 1, seq // DSA_SUB)
    dsa_bias = _toeplitz_tiles(bias_by_dist[:, :A_HEADS], n_sub_tiles, DSA_SUB)
    qi4 = qi.reshape(bsz, seq, IDX_HEADS, IDX_DIM).transpose(0, 2, 1, 3)
    oa = _dsa(qa.reshape(bsz, seq, -1), qi4, wi.reshape(bsz, seq, IDX_HEADS),
              ki.reshape(bsz, seq, IDX_DIM), ckv.reshape(bsz, seq, A_LATENT), kv_g, w_uk, w_uv, dsa_bias)
    nb = seq // MOBA_BLOCK
    kbar = _kbar(kb, bsz, seq).reshape(bsz, nb, B_HEADS, HEAD_DIM).transpose(0, 2, 1, 3)
    n_moba_tiles = min(-(-(MAX_DISTANCE + MOBA_BLOCK) // MOBA_BLOCK) + 1, nb)
    moba_bias = jnp.moveaxis(_toeplitz_tiles(bias_by_dist[:, A_HEADS:], n_moba_tiles, MOBA_BLOCK), 0, 1)
    ob = _moba(_heads(qb, bsz, seq, B_HEADS).astype(BF16), _heads(kb, bsz, seq, B_HEADS).astype(BF16),
               _heads(vb, bsz, seq, B_HEADS).astype(BF16), kbar, moba_bias)
    ob = ob.transpose(0, 2, 1, 3).reshape(bsz * seq, B_HEADS * HEAD_DIM)
    return jnp.concatenate([oa.reshape(bsz * seq, -1), ob], axis=-1)


def _odd_mixer(xf, bsz, seq, w_in, b_f):
    proj = _linear(xf, _pad_cols(w_in).astype(BF16))
    q, k, v, fz = _split_cols(proj[:, :sum(ODD_SPLITS)], ODD_SPLITS)
    fz_t = fz.reshape(bsz, seq, C_HEADS).transpose(0, 2, 1).reshape(bsz * C_HEADS, seq)
    b_rows = jnp.tile(b_f, bsz).reshape(bsz * C_HEADS, 1)
    cum = _cumsum_logsig(fz_t, b_rows).reshape(bsz, C_HEADS, seq)
    o = _fox(_heads(q, bsz, seq, C_HEADS).astype(BF16), _heads(k, bsz, seq, C_HEADS).astype(BF16),
             _heads(v, bsz, seq, C_HEADS).astype(BF16), cum)
    return o.transpose(0, 2, 1, 3).reshape(bsz * seq, C_HEADS * HEAD_DIM)


def kernel(x, even_w_in, even_w_uk, even_w_uv, even_kv_g, even_w_out, odd_w_in, odd_b_f, odd_w_out,
           rel_bias, ln_g, ln_b, router_w, router_b, exp_w_gu, exp_b_gu, exp_w_down, exp_b_down):
    bsz, seq, d = x.shape
    depth = ln_g.shape[0]
    alpha = (2 * depth) ** 0.25
    bias_by_dist = rel_bias[_rel_bucket(jnp.arange(seq))]
    xf = x.reshape(bsz * seq, d)
    for layer in range(depth):
        j = layer // 2
        if layer % 2 == 0:
            a = _even_mixer(xf, bsz, seq, even_w_in[j], even_w_uk[j], even_w_uv[j], even_kv_g[j], bias_by_dist)
            w_out = even_w_out[j]
        else:
            a = _odd_mixer(xf, bsz, seq, odd_w_in[j], odd_b_f[j])
            w_out = odd_w_out[j]
        xf, xb = _linear_ln(a, w_out.astype(BF16), xf, ln_g[layer, 0], ln_b[layer, 0], alpha)
        xf = _moe(xf, xb, router_w[layer], router_b[layer], exp_w_gu[layer].astype(BF16), exp_b_gu[layer],
                  exp_w_down[layer].astype(BF16), exp_b_down[layer], ln_g[layer, 1], ln_b[layer, 1], alpha)
    return xf.reshape(bsz, seq, d)
```

```python
import functools
import math

import numpy as np
import jax
import jax.numpy as jnp
from jax import lax
from jax.experimental import pallas as pl
from jax.experimental.pallas import tpu as pltpu

F32 = jnp.float32
BF16 = jnp.bfloat16
I32 = jnp.int32

HEAD_DIM = 64
ATTN_SCALE = HEAD_DIM ** -0.5
A_HEADS = 8
A_LATENT = 128
IDX_HEADS = 4
IDX_DIM = 64
IDX_TOPK_MAX = 256
B_HEADS = 8
MOBA_BLOCK = 256
MOBA_TOPK = 3
C_HEADS = 16
EVEN_SPLITS = (A_HEADS * HEAD_DIM, A_LATENT, IDX_HEADS * IDX_DIM, IDX_DIM, IDX_HEADS,
               B_HEADS * HEAD_DIM, B_HEADS * HEAD_DIM, B_HEADS * HEAD_DIM)
ODD_SPLITS = (C_HEADS * HEAD_DIM, C_HEADS * HEAD_DIM, C_HEADS * HEAD_DIM, C_HEADS)
NUM_BUCKETS = 32
MAX_EXACT = NUM_BUCKETS // 2
MAX_DISTANCE = 1024
N_EXPERTS = 32
TOP_K = 4
SWIGLU_LIMIT = 7.0
SWIGLU_ALPHA = 1.702
NORM_EPS = 1e-5
LOG2E = math.log2(math.e)

LANES = 128
MASKED = -1e30
KEY_NEG_INF = -2139095041
VMEM_LIMIT = 56 * 1024 * 1024

DSA_Q = 128
DSA_CK = 512
DSA_SUB = 128
MOE_ROWS = 256
FOX_T = 512
MOBA_FAR = 4


def _cparams(sem):
    return pltpu.CompilerParams(dimension_semantics=sem, vmem_limit_bytes=VMEM_LIMIT)


def _split_cols(h, sizes):
    offs = np.cumsum((0,) + tuple(sizes))
    return [h[..., int(offs[i]):int(offs[i + 1])] for i in range(len(sizes))]


def _rel_bucket(dist):
    n = jnp.maximum(dist, 0)
    nf = jnp.maximum(n, MAX_EXACT).astype(F32)
    log_b = MAX_EXACT + (jnp.log(nf / MAX_EXACT) / math.log(MAX_DISTANCE / MAX_EXACT)
                         * (NUM_BUCKETS - MAX_EXACT)).astype(I32)
    return jnp.where(n < MAX_EXACT, n, jnp.minimum(log_b, NUM_BUCKETS - 1))


def _toeplitz_tiles(bias_by_dist, n_tiles, edge):
    i = jnp.arange(edge)[:, None]
    j = jnp.arange(edge)[None, :]
    d = jnp.arange(n_tiles)[:, None, None] * edge + (i - j)[None]
    d = jnp.clip(d, 0, bias_by_dist.shape[0] - 1)
    return jnp.moveaxis(bias_by_dist[d], -1, 1)


def _layer_norm(v, g, b):
    mu = jnp.mean(v, axis=-1, keepdims=True)
    d = v - mu
    var = jnp.mean(d * d, axis=-1, keepdims=True)
    return d * lax.rsqrt(var + NORM_EPS) * g + b


def _linear_kernel(x_ref, w_ref, o_ref):
    o_ref[...] = jnp.dot(x_ref[...].astype(BF16), w_ref[...], preferred_element_type=F32)


def _linear(x, w, tm=256):
    m, k = x.shape
    n = w.shape[1]
    return pl.pallas_call(
        _linear_kernel,
        grid=(m // tm,),
        in_specs=[pl.BlockSpec((tm, k), lambda i: (i, 0)),
                  pl.BlockSpec((k, n), lambda i: (0, 0))],
        out_specs=pl.BlockSpec((tm, n), lambda i: (i, 0)),
        out_shape=jax.ShapeDtypeStruct((m, n), F32),
        compiler_params=_cparams(("parallel",)),
        name="linear",
    )(x, w)


def _linear_ln_kernel(a_ref, w_ref, x_ref, g_ref, b_ref, o_ref, ob_ref, *, alpha):
    m = jnp.dot(a_ref[...].astype(BF16), w_ref[...], preferred_element_type=F32)
    y = _layer_norm(alpha * x_ref[...] + m, g_ref[...], b_ref[...])
    o_ref[...] = y
    ob_ref[...] = y.astype(BF16)


def _linear_ln(a, w, x, g, b, alpha, tm=256):
    m, k = a.shape
    n = w.shape[1]
    return pl.pallas_call(
        functools.partial(_linear_ln_kernel, alpha=alpha),
        grid=(m // tm,),
        in_specs=[pl.BlockSpec((tm, k), lambda i: (i, 0)),
                  pl.BlockSpec((k, n), lambda i: (0, 0)),
                  pl.BlockSpec((tm, n), lambda i: (i, 0)),
                  pl.BlockSpec((1, n), lambda i: (0, 0)),
                  pl.BlockSpec((1, n), lambda i: (0, 0))],
        out_specs=[pl.BlockSpec((tm, n), lambda i: (i, 0)),
                   pl.BlockSpec((tm, n), lambda i: (i, 0))],
        out_shape=[jax.ShapeDtypeStruct((m, n), F32), jax.ShapeDtypeStruct((m, n), BF16)],
        compiler_params=_cparams(("parallel",)),
        name="linear_ln",
    )(a, w, x, g.reshape(1, n), b.reshape(1, n))


def _float_key(s):
    bits = lax.bitcast_convert_type(s, I32)
    return bits ^ ((bits >> 31) & 0x7FFFFFFF)


def _split_hi_lo(v):
    hi = v.astype(BF16)
    return hi, (v - hi.astype(F32)).astype(BF16)


def _dsa_kernel(qa_ref, qi_ref, wi_ref, kidx_ref, ckv_ref, kvg_ref, wuk_ref, wuv_ref, bias_ref,
                o_ref, key_ref, cn_ref, kcat_ref, m_ref, acc_ref, *, seq, topk, idx_bits):
    qb = pl.program_id(1)
    s0 = qb * DSA_Q
    nc = (s0 + DSA_Q + DSA_CK - 1) // DSA_CK
    n_sub = DSA_CK // LANES
    n_bias = bias_ref.shape[0]

    @pl.when(qb == 0)
    def _():
        ones_col = jnp.where(lax.broadcasted_iota(I32, (DSA_CK, LANES), 1) == 0, 1.0, 0.0).astype(BF16)

        def prep_chunk(c, carry):
            rows = pl.ds(pl.multiple_of(c * DSA_CK, DSA_CK), DSA_CK)
            v = ckv_ref[0, rows, :]
            y = v * lax.rsqrt(jnp.mean(v * v, axis=-1, keepdims=True) + NORM_EPS) * kvg_ref[...]
            cn_ref[rows, :] = jnp.concatenate([y.astype(BF16), ones_col], axis=1)
            k_hi, k_lo = _split_hi_lo(kidx_ref[0, rows, :])
            kcat_ref[rows, :] = jnp.concatenate([k_hi, k_lo, k_hi, k_lo], axis=1)
            return carry
        lax.fori_loop(0, seq // DSA_CK, prep_chunk, 0)

    t_pos = s0 + lax.broadcasted_iota(I32, (DSA_Q, LANES), 0)
    lane = lax.broadcasted_iota(I32, (DSA_Q, LANES), 1)

    q_hi, q_lo = _split_hi_lo(qi_ref[0].reshape(IDX_HEADS * DSA_Q, IDX_DIM))
    qcat = jnp.concatenate([q_hi, q_hi, q_lo, q_lo], axis=1)
    wi = wi_ref[0]

    def score_chunk(c, carry):
        rows = pl.ds(pl.multiple_of(c * DSA_CK, DSA_CK), DSA_CK)
        r = lax.dot_general(qcat, kcat_ref[rows, :], (((1,), (1,)), ((), ())),
                            preferred_element_type=F32)
        r = jnp.maximum(r, 0.0)
        sc = wi[:, 0:1] * r[0:DSA_Q]
        for j in range(1, IDX_HEADS):
            sc = sc + wi[:, j:j + 1] * r[j * DSA_Q:(j + 1) * DSA_Q]
        sc = jnp.where(sc == 0.0, 0.0, sc)
        for u in range(n_sub):
            s_pos = c * DSA_CK + u * LANES + lane
            su = sc[:, u * LANES:(u + 1) * LANES]
            key_ref[c, :, u * LANES:(u + 1) * LANES] = jnp.where(
                s_pos <= t_pos, _float_key(su), KEY_NEG_INF)
        return carry
    lax.fori_loop(0, nc, score_chunk, 0)

    def count(pred_fn):
        def body(c, acc):
            k = key_ref[c]
            for u in range(n_sub):
                acc = acc + pred_fn(k[:, u * LANES:(u + 1) * LANES], c * DSA_CK + u * LANES + lane)
            return acc
        acc = lax.fori_loop(0, nc, body, jnp.zeros((DSA_Q, LANES), F32))
        return jnp.sum(acc, axis=1, keepdims=True)

    kf = float(topk)
    cnt0 = count(lambda k, sp: jnp.where(k >= 0, 1.0, 0.0))
    cand0 = jnp.where(cnt0 >= kf, 0, -2147483648).astype(I32)

    def thr_step(it, cand):
        test = cand | jnp.left_shift(jnp.int32(1), 30 - it)
        cnt = count(lambda k, sp: jnp.where(k >= test, 1.0, 0.0))
        return jnp.where(cnt >= kf, test, cand)
    thr = lax.fori_loop(0, 31, thr_step, cand0)

    n_gt = count(lambda k, sp: jnp.where(k > thr, 1.0, 0.0))
    n_ge = count(lambda k, sp: jnp.where(k >= thr, 1.0, 0.0))
    need = kf - n_gt

    def find_cut():
        def cut_step(it, cut):
            test = cut | jnp.left_shift(jnp.int32(1), idx_bits - 1 - it)
            cnt = count(lambda k, sp: jnp.where(k == thr, jnp.where(sp < test, 1.0, 0.0), 0.0))
            return jnp.where(cnt < need, test, cut)
        return lax.fori_loop(0, idx_bits, cut_step, jnp.zeros((DSA_Q, 1), I32))

    surplus = jnp.where(n_ge > kf, jnp.where(thr != KEY_NEG_INF, 1.0, 0.0), 0.0)
    cut = lax.cond(jnp.max(surplus) > 0.5, find_cut, lambda: jnp.full((DSA_Q, 1), seq, I32))

    ql = jnp.dot(qa_ref[0].astype(BF16), wuk_ref[...], preferred_element_type=F32) * (ATTN_SCALE * LOG2E)
    ql = jnp.concatenate([ql[:, h * A_LATENT:(h + 1) * A_LATENT] for h in range(A_HEADS)],
                         axis=0).astype(BF16)
    m_ref[...] = jnp.full(m_ref.shape, MASKED, F32)
    acc_ref[...] = jnp.zeros(acc_ref.shape, F32)

    def attn_chunk(c, near):
        rows = pl.ds(pl.multiple_of(c * DSA_CK, DSA_CK), DSA_CK)
        cc = cn_ref[rows, :]
        lg = lax.dot_general(ql, cc[:, :A_LATENT], (((1,), (1,)), ((), ())), preferred_element_type=F32)
        k = key_ref[c]
        mask_parts, bias_parts = [], []
        for u in range(n_sub):
            s_pos = c * DSA_CK + u * LANES + lane
            ku = k[:, u * LANES:(u + 1) * LANES]
            take = jnp.where(ku > thr, 0.0, jnp.where(ku == thr, jnp.where(s_pos <= cut, 0.0, MASKED), MASKED))
            mask_parts.append(jnp.where(s_pos <= t_pos, take, MASKED))
            if near:
                bias_parts.append(bias_ref[jnp.clip(qb - (c * n_sub + u), 0, n_bias - 1)])
        lg = lg.reshape(A_HEADS, DSA_Q, DSA_CK) + jnp.concatenate(mask_parts, axis=1)[None]
        if near:
            lg = lg + jnp.concatenate(bias_parts, axis=2)
        lg = lg.reshape(A_HEADS * DSA_Q, DSA_CK)
        m_prev = m_ref[...]
        m_new = jnp.maximum(m_prev, jnp.max(lg, axis=1, keepdims=True))
        a = jnp.exp2(m_prev - m_new)
        p = jnp.exp2(lg - jnp.concatenate([m_new] * n_sub, axis=1))
        acc_ref[...] = (jnp.concatenate([a, a], axis=1) * acc_ref[...]
                        + jnp.dot(p.astype(BF16), cc, preferred_element_type=F32))
        m_ref[...] = m_new

    n_far = jnp.clip((qb - (n_bias - 2)) // n_sub, 0, nc) if n_bias * DSA_SUB > MAX_DISTANCE else 0

    def far_chunk(c, carry):
        attn_chunk(c, False)
        return carry

    def near_chunk(c, carry):
        attn_chunk(c, True)
        return carry
    lax.fori_loop(0, n_far, far_chunk, 0)
    lax.fori_loop(n_far, nc, near_chunk, 0)

    acc = acc_ref[...]
    o_lat = (acc[:, :A_LATENT] / acc[:, A_LATENT:A_LATENT + 1]).astype(BF16)
    out = jnp.dot(o_lat[0:DSA_Q], wuv_ref[0], preferred_element_type=F32)
    for h in range(1, A_HEADS):
        out = out + jnp.dot(o_lat[h * DSA_Q:(h + 1) * DSA_Q], wuv_ref[h], preferred_element_type=F32)
    o_ref[0] = out


def _dsa(qa, qi4, wi, kidx, ckv, kv_g, w_uk, w_uv, bias_tiles):
    bsz, seq, _ = qa.shape
    topk = min(IDX_TOPK_MAX, seq // 4)
    idx_bits = max(1, int(math.ceil(math.log2(seq))))
    hd = A_HEADS * HEAD_DIM
    wuk_bd = jnp.zeros((hd, A_HEADS * A_LATENT), F32)
    wuv_ex = jnp.zeros((A_HEADS, A_LATENT, hd), F32)
    for h in range(A_HEADS):
        wuk_bd = wuk_bd.at[h * HEAD_DIM:(h + 1) * HEAD_DIM, h * A_LATENT:(h + 1) * A_LATENT].set(w_uk[h])
        wuv_ex = wuv_ex.at[h, :, h * HEAD_DIM:(h + 1) * HEAD_DIM].set(w_uv[h])
    n_bias = bias_tiles.shape[0]
    kern = functools.partial(_dsa_kernel, seq=seq, topk=topk, idx_bits=idx_bits)
    return pl.pallas_call(
        kern,
        grid=(bsz, seq // DSA_Q),
        in_specs=[
            pl.BlockSpec((1, DSA_Q, hd), lambda b, q: (b, q, 0)),
            pl.BlockSpec((1, IDX_HEADS, DSA_Q, IDX_DIM), lambda b, q: (b, 0, q, 0)),
            pl.BlockSpec((1, DSA_Q, IDX_HEADS), lambda b, q: (b, q, 0)),
            pl.BlockSpec((1, seq, IDX_DIM), lambda b, q: (b, 0, 0)),
            pl.BlockSpec((1, seq, A_LATENT), lambda b, q: (b, 0, 0)),
            pl.BlockSpec((1, A_LATENT), lambda b, q: (0, 0)),
            pl.BlockSpec((hd, A_HEADS * A_LATENT), lambda b, q: (0, 0)),
            pl.BlockSpec((A_HEADS, A_LATENT, hd), lambda b, q: (0, 0, 0)),
            pl.BlockSpec((n_bias, A_HEADS, DSA_SUB, DSA_SUB), lambda b, q: (0, 0, 0, 0)),
        ],
        out_specs=pl.BlockSpec((1, DSA_Q, hd), lambda b, q: (b, q, 0)),
        out_shape=jax.ShapeDtypeStruct((bsz, seq, hd), F32),
        scratch_shapes=[
            pltpu.VMEM((seq // DSA_CK, DSA_Q, DSA_CK), I32),
            pltpu.VMEM((seq, 2 * A_LATENT), BF16),
            pltpu.VMEM((seq, 4 * IDX_DIM), BF16),
            pltpu.VMEM((A_HEADS * DSA_Q, LANES), F32),
            pltpu.VMEM((A_HEADS * DSA_Q, 2 * A_LATENT), F32),
        ],
        compiler_params=_cparams(("arbitrary", "arbitrary")),
        name="dsa",
    )(qa, qi4, wi, kidx, ckv, kv_g.reshape(1, A_LATENT), wuk_bd.astype(BF16), wuv_ex.astype(BF16),
      bias_tiles)


def _kbar_kernel(k_ref, o_ref):
    o_ref[0] = jnp.mean(k_ref[...], axis=0, keepdims=True)


def _kbar(kb, bsz, seq):
    nb = seq // MOBA_BLOCK
    hd = kb.shape[1]
    return pl.pallas_call(
        _kbar_kernel,
        grid=(bsz * nb,),
        in_specs=[pl.BlockSpec((MOBA_BLOCK, hd), lambda i: (i, 0))],
        out_specs=pl.BlockSpec((1, 1, hd), lambda i: (i, 0, 0)),
        out_shape=jax.ShapeDtypeStruct((bsz * nb, 1, hd), F32),
        compiler_params=_cparams(("parallel",)),
        name="kbar",
    )(kb)


def _moba_kernel(q_ref, k_ref, v_ref, kbar_ref, strip_ref, o_ref, m_ref, acc_ref, *, nb, n_near):
    qi = pl.program_id(2)
    nh = q_ref.shape[1]
    tq = MOBA_BLOCK
    lane = lax.broadcasted_iota(I32, (tq, LANES), 1)
    lanef = lane.astype(F32)
    blk = lane - HEAD_DIM
    past = jnp.where(blk >= 0, jnp.where(blk < qi, 1.0, 0.0), 0.0)

    qs = []
    for h in range(nh):
        q = q_ref[0, h]
        gate = lax.dot_general(q[:, :HEAD_DIM], kbar_ref[0, h], (((1,), (1,)), ((), ())),
                               preferred_element_type=F32, precision=lax.Precision.HIGHEST)
        g = jnp.where(past > 0.5, gate, -jnp.inf)
        keep = jnp.where(blk == qi, 1.0, 0.0)
        for _ in range(MOBA_TOPK):
            mx = jnp.max(g, axis=1, keepdims=True)
            ix = jnp.min(jnp.where(g == mx, lanef, float(LANES)), axis=1, keepdims=True)
            hit = jnp.where(lanef == ix, jnp.where(mx > -jnp.inf, 1.0, 0.0), 0.0)
            keep = jnp.maximum(keep, hit)
            g = jnp.where(hit > 0.5, -jnp.inf, g)
        mask = jnp.where(blk >= 0, jnp.where(blk < nb, jnp.where(keep > 0.5, 0.0, MASKED), 0.0), 0.0)
        qs.append((q * (ATTN_SCALE * LOG2E) + mask).astype(BF16))

    m_ref[...] = jnp.full(m_ref.shape, MASKED, F32)
    acc_ref[...] = jnp.zeros(acc_ref.shape, F32)

    def update(h, s, rows):
        m_prev = m_ref[h]
        m_new = jnp.maximum(m_prev, jnp.max(s, axis=1, keepdims=True))
        a = jnp.exp2(m_prev - m_new)
        p = jnp.exp2(s - jnp.concatenate([m_new] * (s.shape[1] // LANES), axis=1))
        acc_ref[h] = a * acc_ref[h] + jnp.dot(p.astype(BF16), v_ref[0, h, rows, :],
                                              preferred_element_type=F32)
        m_ref[h] = m_new

    def logits(h, rows):
        return lax.dot_general(qs[h], k_ref[0, h, rows, :], (((1,), (1,)), ((), ())),
                               preferred_element_type=F32)

    first = jnp.maximum(qi - (n_near - 1), 0)
    n_far = first // MOBA_FAR

    def far_chunk(i, carry):
        rows = pl.ds(pl.multiple_of(i * (MOBA_FAR * tq), MOBA_FAR * tq), MOBA_FAR * tq)
        for h in range(nh):
            update(h, logits(h, rows), rows)
        return carry
    lax.fori_loop(0, n_far, far_chunk, 0)

    def far_block(n, carry):
        rows = pl.ds(pl.multiple_of(n * tq, tq), tq)
        for h in range(nh):
            update(h, logits(h, rows), rows)
        return carry
    lax.fori_loop(n_far * MOBA_FAR, first, far_block, 0)

    rows = pl.ds(pl.multiple_of(first * tq, tq), n_near * tq)
    outs = []
    for h in range(nh):
        update(h, logits(h, rows) + strip_ref[h, 0], rows)
        acc = acc_ref[h]
        outs.append(acc[:, :HEAD_DIM] / acc[:, HEAD_DIM:HEAD_DIM + 1])
    o_ref[0] = jnp.concatenate(outs, axis=1)


def _moba_strips(tiles):
    nh, n, e, _ = tiles.shape
    causal = jnp.arange(e)[None, :] <= jnp.arange(e)[:, None]
    variants = []
    for v in range(n):
        blocks = []
        for j in range(n):
            if j < v:
                blocks.append(tiles[:, v - j])
            elif j == v:
                blocks.append(jnp.where(causal[None], tiles[:, 0], MASKED))
            else:
                blocks.append(jnp.full((nh, e, e), MASKED, F32))
        variants.append(jnp.concatenate(blocks, axis=-1))
    return jnp.stack(variants, axis=1)


def _moba(q_pad, k_aug, v_aug, kbar_pad, bias_tiles):
    bsz, nh, seq, _ = q_pad.shape
    nb = seq // MOBA_BLOCK
    assert HEAD_DIM + nb <= LANES
    n_near = bias_tiles.shape[1]
    assert n_near == nb or n_near * MOBA_BLOCK - (MOBA_BLOCK - 1) >= MAX_DISTANCE
    strips = _moba_strips(bias_tiles)
    hp = LANES // HEAD_DIM
    return pl.pallas_call(
        functools.partial(_moba_kernel, nb=nb, n_near=n_near),
        grid=(bsz, nh // hp, nb),
        in_specs=[
            pl.BlockSpec((1, hp, MOBA_BLOCK, LANES), lambda b, h, i: (b, h, i, 0)),
            pl.BlockSpec((1, hp, seq, LANES), lambda b, h, i: (b, h, 0, 0)),
            pl.BlockSpec((1, hp, seq, LANES), lambda b, h, i: (b, h, 0, 0)),
            pl.BlockSpec((1, hp, LANES, HEAD_DIM), lambda b, h, i: (b, h, 0, 0)),
            pl.BlockSpec((hp, 1, MOBA_BLOCK, n_near * MOBA_BLOCK),
                         lambda b, h, i: (h, jnp.minimum(i, n_near - 1), 0, 0)),
        ],
        out_specs=pl.BlockSpec((1, MOBA_BLOCK, LANES), lambda b, h, i: (b, i, h)),
        out_shape=jax.ShapeDtypeStruct((bsz, seq, nh * HEAD_DIM), F32),
        scratch_shapes=[
            pltpu.VMEM((hp, MOBA_BLOCK, LANES), F32),
            pltpu.VMEM((hp, MOBA_BLOCK, LANES), F32),
        ],
        compiler_params=_cparams(("parallel", "parallel", "arbitrary")),
        name="moba",
    )(q_pad, k_aug, v_aug, kbar_pad, strips)


def _cumsum_kernel(f_ref, b_ref, o_ref, *, seq):
    z = f_ref[...] + b_ref[...]
    y = jnp.minimum(z, 0.0) - jnp.log1p(jnp.exp(-jnp.abs(z)))
    lane = lax.broadcasted_iota(I32, y.shape, 1)
    shift = 1
    while shift < seq:
        y = y + jnp.where(lane >= shift, pltpu.roll(y, shift, axis=1), 0.0)
        shift *= 2
    y2 = y * LOG2E
    hi = y2.astype(BF16).astype(F32)
    r1 = y2 - hi
    mid = r1.astype(BF16).astype(F32)
    o_ref[0] = hi
    o_ref[1] = mid
    o_ref[2] = (r1 - mid).astype(BF16).astype(F32)


def _cumsum_logsig(fz_t, bias_rows):
    rows, seq = fz_t.shape
    return pl.pallas_call(
        functools.partial(_cumsum_kernel, seq=seq),
        grid=(1,),
        in_specs=[pl.BlockSpec((rows, seq), lambda i: (0, 0)),
                  pl.BlockSpec((rows, 1), lambda i: (0, 0))],
        out_specs=pl.BlockSpec((3, rows, seq), lambda i: (0, 0, 0)),
        out_shape=jax.ShapeDtypeStruct((3, rows, seq), F32),
        compiler_params=_cparams(("arbitrary",)),
        name="cumsum_logsig",
    )(fz_t, bias_rows)


def _fox_kernel(q_ref, k_ref, v_ref, o_ref, m_ref, acc_ref):
    qi = pl.program_id(2)
    t = FOX_T
    nh = q_ref.shape[1]
    lane = lax.broadcasted_iota(I32, (1, LANES), 1)
    qscale = jnp.where(lane < HEAD_DIM, ATTN_SCALE * LOG2E, 1.0)
    qs = [(q_ref[0, h] * qscale).astype(BF16) for h in range(nh)]
    m_ref[...] = jnp.full(m_ref.shape, MASKED, F32)
    acc_ref[...] = jnp.zeros(acc_ref.shape, F32)

    def update(h, n, diagonal):
        rows = pl.ds(pl.multiple_of(n * t, t), t)
        s = lax.dot_general(qs[h], k_ref[0, h, rows, :], (((1,), (1,)), ((), ())),
                            preferred_element_type=F32)
        if diagonal:
            ri = lax.broadcasted_iota(I32, (t, t), 0)
            ci = lax.broadcasted_iota(I32, (t, t), 1)
            s = jnp.where(ci <= ri, s, MASKED)
        m_prev = m_ref[h]
        m_new = jnp.maximum(m_prev, jnp.max(s, axis=1, keepdims=True))
        a = jnp.exp2(m_prev - m_new)
        p = jnp.exp2(s - jnp.concatenate([m_new] * (t // LANES), axis=1))
        acc_ref[h] = a * acc_ref[h] + jnp.dot(p.astype(BF16), v_ref[0, h, rows, :],
                                              preferred_element_type=F32)
        m_ref[h] = m_new

    def past_block(n, carry):
        for h in range(nh):
            update(h, n, False)
        return carry
    lax.fori_loop(0, qi, past_block, 0)
    outs = []
    for h in range(nh):
        update(h, qi, True)
        acc = acc_ref[h]
        outs.append(acc[:, :HEAD_DIM] / acc[:, HEAD_DIM:HEAD_DIM + 1])
    o_ref[0] = jnp.concatenate(outs, axis=1)


def _fox(q_aug, k_aug, v_aug):
    bsz, nh, seq, _ = q_aug.shape
    t = FOX_T
    hp = LANES // HEAD_DIM
    return pl.pallas_call(
        _fox_kernel,
        grid=(bsz, nh // hp, seq // t),
        in_specs=[
            pl.BlockSpec((1, hp, t, LANES), lambda b, h, i: (b, h, i, 0)),
            pl.BlockSpec((1, hp, seq, LANES), lambda b, h, i: (b, h, 0, 0)),
            pl.BlockSpec((1, hp, seq, LANES), lambda b, h, i: (b, h, 0, 0)),
        ],
        out_specs=pl.BlockSpec((1, t, LANES), lambda b, h, i: (b, i, h)),
        out_shape=jax.ShapeDtypeStruct((bsz, seq, nh * HEAD_DIM), F32),
        scratch_shapes=[
            pltpu.VMEM((hp, t, LANES), F32),
            pltpu.VMEM((hp, t, LANES), F32),
        ],
        compiler_params=_cparams(("parallel", "parallel", "arbitrary")),
        name="fox",
    )(q_aug, k_aug, v_aug)


def _router_kernel(x_ref, w_ref, b_ref, e_ref, g_ref):
    lg = jnp.dot(x_ref[...], w_ref[...], preferred_element_type=F32,
                 precision=lax.Precision.HIGHEST) + b_ref[...]
    tm, ne = lg.shape
    col = lax.broadcasted_iota(I32, (tm, ne), 1).astype(F32)
    vals, idxs = [], []
    for _ in range(TOP_K):
        mx = jnp.max(lg, axis=1, keepdims=True)
        ix = jnp.min(jnp.where(lg == mx, col, float(ne)), axis=1, keepdims=True)
        vals.append(mx)
        idxs.append(ix)
        lg = jnp.where(col == ix, -jnp.inf, lg)
    ex = [jnp.exp(v - vals[0]) for v in vals]
    den = ex[0]
    for e in ex[1:]:
        den = den + e
    kcol = lax.broadcasted_iota(I32, (tm, TOP_K), 1)
    e_out = jnp.zeros((tm, TOP_K), I32)
    g_out = jnp.zeros((tm, TOP_K), F32)
    for k in range(TOP_K):
        e_out = jnp.where(kcol == k, idxs[k].astype(I32), e_out)
        g_out = jnp.where(kcol == k, ex[k] / den, g_out)
    e_ref[...] = e_out
    g_ref[...] = g_out


def _router(x, w, b, tm=512):
    n, d = x.shape
    ne = w.shape[1]
    return pl.pallas_call(
        _router_kernel,
        grid=(n // tm,),
        in_specs=[pl.BlockSpec((tm, d), lambda i: (i, 0)),
                  pl.BlockSpec((d, ne), lambda i: (0, 0)),
                  pl.BlockSpec((1, ne), lambda i: (0, 0))],
        out_specs=[pl.BlockSpec((tm, TOP_K), lambda i: (i, 0)),
                   pl.BlockSpec((tm, TOP_K), lambda i: (i, 0))],
        out_shape=[jax.ShapeDtypeStruct((n, TOP_K), I32), jax.ShapeDtypeStruct((n, TOP_K), F32)],
        compiler_params=_cparams(("parallel",)),
        name="router",
    )(x, w, b.reshape(1, ne))


def _expert_kernel(be_ref, nu_ref, xs_ref, wgu_ref, bgu_ref, wd_ref, bd_ref, o_ref):
    i = pl.program_id(0)
    d_ff = wd_ref.shape[1]

    @pl.when(i < nu_ref[0])
    def _():
        h = jnp.dot(xs_ref[...], wgu_ref[0], preferred_element_type=F32) + bgu_ref[0]
        g = jnp.minimum(h[:, :d_ff], SWIGLU_LIMIT)
        u = jnp.clip(h[:, d_ff:], -SWIGLU_LIMIT, SWIGLU_LIMIT)
        act = (u + 1.0) * g * (1.0 / (1.0 + jnp.exp(-SWIGLU_ALPHA * g)))
        o_ref[...] = jnp.dot(act.astype(BF16), wd_ref[0], preferred_element_type=F32) + bd_ref[0]

    @pl.when(i >= nu_ref[0])
    def _():
        o_ref[...] = jnp.zeros(o_ref.shape, o_ref.dtype)


def _experts(xs, block_e, n_used, w_gu, b_gu, w_down, b_down):
    n_rows, d = xs.shape
    ne, _, f2 = w_gu.shape
    d_ff = f2 // 2
    n_blocks = n_rows // MOE_ROWS
    grid_spec = pltpu.PrefetchScalarGridSpec(
        num_scalar_prefetch=2,
        grid=(n_blocks,),
        in_specs=[
            pl.BlockSpec((MOE_ROWS, d), lambda i, be, nu: (i, 0)),
            pl.BlockSpec((1, d, f2), lambda i, be, nu: (be[i], 0, 0)),
            pl.BlockSpec((1, 1, f2), lambda i, be, nu: (be[i], 0, 0)),
            pl.BlockSpec((1, d_ff, d), lambda i, be, nu: (be[i], 0, 0)),
            pl.BlockSpec((1, 1, d), lambda i, be, nu: (be[i], 0, 0)),
        ],
        out_specs=pl.BlockSpec((MOE_ROWS, d), lambda i, be, nu: (i, 0)),
    )
    return pl.pallas_call(
        _expert_kernel,
        grid_spec=grid_spec,
        out_shape=jax.ShapeDtypeStruct((n_rows, d), F32),
        compiler_params=_cparams(("arbitrary",)),
        name="experts",
    )(block_e, n_used, xs, w_gu, b_gu.reshape(ne, 1, f2), w_down, b_down.reshape(ne, 1, d))


def _combine_ln_kernel(y_ref, gt_ref, x_ref, g_ref, b_ref, o_ref, *, alpha):
    gt = gt_ref[...]
    y = y_ref[0] * gt[:, 0:1]
    for k in range(1, TOP_K):
        y = y + y_ref[k] * gt[:, k:k + 1]
    o_ref[...] = _layer_norm(alpha * x_ref[...] + y, g_ref[...], b_ref[...])


def _combine_ln(yg, gates, x, g, b, alpha, tm=256):
    n, d = x.shape
    return pl.pallas_call(
        functools.partial(_combine_ln_kernel, alpha=alpha),
        grid=(n // tm,),
        in_specs=[pl.BlockSpec((TOP_K, tm, d), lambda i: (0, i, 0)),
                  pl.BlockSpec((tm, TOP_K), lambda i: (i, 0)),
                  pl.BlockSpec((tm, d), lambda i: (i, 0)),
                  pl.BlockSpec((1, d), lambda i: (0, 0)),
                  pl.BlockSpec((1, d), lambda i: (0, 0))],
        out_specs=pl.BlockSpec((tm, d), lambda i: (i, 0)),
        out_shape=jax.ShapeDtypeStruct((n, d), F32),
        compiler_params=_cparams(("parallel",)),
        name="combine_ln",
    )(yg, gates, x, g.reshape(1, d), b.reshape(1, d))


def _moe(x, xb, router_w, router_b, w_gu, b_gu, w_down, b_down, ln_g, ln_b, alpha):
    n, d = x.shape
    top_e, gates = _router(x, router_w, router_b)
    nk = n * TOP_K
    flat_e = top_e.reshape(-1)
    order = jnp.argsort(flat_e)
    se = flat_e[order]
    counts = jnp.bincount(flat_e, length=N_EXPERTS)
    padded = (counts + MOE_ROWS - 1) // MOE_ROWS * MOE_ROWS
    start = jnp.cumsum(counts) - counts
    pend = jnp.cumsum(padded)
    pstart = pend - padded
    dest = (pstart[se] + (jnp.arange(nk) - start[se])).astype(I32)
    n_blocks = nk // MOE_ROWS + N_EXPERTS
    n_rows = n_blocks * MOE_ROWS
    row_tok = jnp.zeros((n_rows,), I32).at[dest].set((order // TOP_K).astype(I32))
    pos = jnp.zeros((nk,), I32).at[order].set(dest)
    block_e = jnp.minimum(jnp.searchsorted(pend, jnp.arange(n_blocks) * MOE_ROWS, side='right'),
                          N_EXPERTS - 1).astype(I32)
    n_used = (pend[-1] // MOE_ROWS).astype(I32).reshape(1)
    xs = jnp.take(xb, row_tok, axis=0)
    y_rows = _experts(xs, block_e, n_used, w_gu, b_gu, w_down, b_down)
    pos_t = pos.reshape(n, TOP_K).T.reshape(-1)
    yg = jnp.take(y_rows, pos_t, axis=0).reshape(TOP_K, n, d)
    return _combine_ln(yg, gates, x, ln_g, ln_b, alpha)


def _pad_cols(w, mult=LANES):
    n = w.shape[1]
    pad = (-n) % mult
    return jnp.pad(w, ((0, 0), (0, pad))) if pad else w


def _heads(t, bsz, seq, nh):
    return t.reshape(bsz, seq, nh, HEAD_DIM).transpose(0, 2, 1, 3)


def _even_mixer(xf, bsz, seq, w_in, w_uk, w_uv, kv_g, bias_by_dist):
    proj = _linear(xf, _pad_cols(w_in).astype(BF16))
    qa, ckv, qi, ki, wi, qb, kb, vb = _split_cols(proj[:, :sum(EVEN_SPLITS)], EVEN_SPLITS)
    n_sub_tiles = min(-(-(MAX_DISTANCE) // DSA_SUB) + 1, seq // DSA_SUB)
    dsa_bias = _toeplitz_tiles(bias_by_dist[:, :A_HEADS], n_sub_tiles, DSA_SUB)
    qi4 = qi.reshape(bsz, seq, IDX_HEADS, IDX_DIM).transpose(0, 2, 1, 3)
    oa = _dsa(qa.reshape(bsz, seq, -1), qi4, wi.reshape(bsz, seq, IDX_HEADS),
              ki.reshape(bsz, seq, IDX_DIM), ckv.reshape(bsz, seq, A_LATENT), kv_g, w_uk, w_uv, dsa_bias)
    nb = seq // MOBA_BLOCK
    kbar = _kbar(kb, bsz, seq).reshape(bsz, nb, B_HEADS, HEAD_DIM).transpose(0, 2, 1, 3)
    kbar_pad = jnp.pad(kbar, ((0, 0), (0, 0), (HEAD_DIM, LANES - HEAD_DIM - nb), (0, 0)))
    n_moba_tiles = min(-(-(MAX_DISTANCE + MOBA_BLOCK) // MOBA_BLOCK), nb)
    moba_bias = jnp.moveaxis(_toeplitz_tiles(bias_by_dist[:, A_HEADS:], n_moba_tiles, MOBA_BLOCK), 0, 1)
    lead = (bsz, B_HEADS, seq)
    block_1hot = (jnp.arange(seq)[:, None] // MOBA_BLOCK == jnp.arange(nb)[None, :]).astype(F32)
    q_pad = jnp.concatenate([_heads(qb, bsz, seq, B_HEADS), jnp.zeros(lead + (LANES - HEAD_DIM,), F32)], axis=-1)
    k_aug = jnp.concatenate([_heads(kb, bsz, seq, B_HEADS), jnp.broadcast_to(block_1hot, lead + (nb,)),
                             jnp.zeros(lead + (LANES - HEAD_DIM - nb,), F32)], axis=-1).astype(BF16)
    v_aug = jnp.concatenate([_heads(vb, bsz, seq, B_HEADS), jnp.ones(lead + (1,), F32),
                             jnp.zeros(lead + (LANES - HEAD_DIM - 1,), F32)], axis=-1).astype(BF16)
    ob = _moba(q_pad, k_aug, v_aug, kbar_pad, moba_bias)
    return jnp.concatenate([oa.reshape(bsz * seq, -1), ob.reshape(bsz * seq, -1)], axis=-1)


def _odd_mixer(xf, bsz, seq, w_in, b_f):
    proj = _linear(xf, _pad_cols(w_in).astype(BF16))
    q, k, v, fz = _split_cols(proj[:, :sum(ODD_SPLITS)], ODD_SPLITS)
    fz_t = fz.reshape(bsz, seq, C_HEADS).transpose(0, 2, 1).reshape(bsz * C_HEADS, seq)
    b_rows = jnp.tile(b_f, bsz).reshape(bsz * C_HEADS, 1)
    fparts = _cumsum_logsig(fz_t, b_rows).reshape(3, bsz, C_HEADS, seq)
    fparts = jnp.moveaxis(fparts, 0, -1)
    ones = jnp.ones_like(fparts)
    pad = LANES - HEAD_DIM - 6
    zeros = jnp.zeros(fparts.shape[:-1] + (pad,), F32)
    q_aug = jnp.concatenate([_heads(q, bsz, seq, C_HEADS), fparts, ones, zeros], axis=-1)
    k_aug = jnp.concatenate([_heads(k, bsz, seq, C_HEADS), ones, -fparts, zeros], axis=-1).astype(BF16)
    v_aug = jnp.concatenate([_heads(v, bsz, seq, C_HEADS), ones[..., :1],
                             jnp.zeros(fparts.shape[:-1] + (LANES - HEAD_DIM - 1,), F32)], axis=-1).astype(BF16)
    return _fox(q_aug, k_aug, v_aug).reshape(bsz * seq, C_HEADS * HEAD_DIM)


def kernel(x, even_w_in, even_w_uk, even_w_uv, even_kv_g, even_w_out, odd_w_in, odd_b_f, odd_w_out,
           rel_bias, ln_g, ln_b, router_w, router_b, exp_w_gu, exp_b_gu, exp_w_down, exp_b_down):
    bsz, seq, d = x.shape
    depth = ln_g.shape[0]
    alpha = (2 * depth) ** 0.25
    bias_by_dist = (rel_bias[_rel_bucket(jnp.arange(seq))] - rel_bias[NUM_BUCKETS - 1]) * LOG2E
    xf = x.reshape(bsz * seq, d)
    for layer in range(depth):
        j = layer // 2
        if layer % 2 == 0:
            a = _even_mixer(xf, bsz, seq, even_w_in[j], even_w_uk[j], even_w_uv[j], even_kv_g[j], bias_by_dist)
            w_out = even_w_out[j]
        else:
            a = _odd_mixer(xf, bsz, seq, odd_w_in[j], odd_b_f[j])
            w_out = odd_w_out[j]
        xf, xb = _linear_ln(a, w_out.astype(BF16), xf, ln_g[layer, 0], ln_b[layer, 0], alpha)
        xf = _moe(xf, xb, router_w[layer], router_b[layer], exp_w_gu[layer].astype(BF16), exp_b_gu[layer],
                  exp_w_down[layer].astype(BF16), exp_b_down[layer], ln_g[layer, 1], ln_b[layer, 1], alpha)
    return xf.reshape(bsz, seq, d)
```

```python
import functools
import math

import numpy as np
import jax
import jax.numpy as jnp
from jax import lax
from jax.experimental import pallas as pl
from jax.experimental.pallas import tpu as pltpu

F32 = jnp.float32
BF16 = jnp.bfloat16
I32 = jnp.int32

HEAD_DIM = 64
ATTN_SCALE = HEAD_DIM ** -0.5
A_HEADS = 8
A_LATENT = 128
IDX_HEADS = 4
IDX_DIM = 64
IDX_TOPK_MAX = 256
B_HEADS = 8
MOBA_BLOCK = 256
MOBA_TOPK = 3
C_HEADS = 16
EVEN_SPLITS = (A_HEADS * HEAD_DIM, A_LATENT, IDX_HEADS * IDX_DIM, IDX_DIM, IDX_HEADS,
               B_HEADS * HEAD_DIM, B_HEADS * HEAD_DIM, B_HEADS * HEAD_DIM)
ODD_SPLITS = (C_HEADS * HEAD_DIM, C_HEADS * HEAD_DIM, C_HEADS * HEAD_DIM, C_HEADS)
NUM_BUCKETS = 32
MAX_EXACT = NUM_BUCKETS // 2
MAX_DISTANCE = 1024
N_EXPERTS = 32
TOP_K = 4
SWIGLU_LIMIT = 7.0
SWIGLU_ALPHA = 1.702
NORM_EPS = 1e-5
LOG2E = math.log2(math.e)

LANES = 128
MASKED = -1e30
KEY_NEG_INF = -2139095041
VMEM_LIMIT = 56 * 1024 * 1024

DSA_Q = 128
DSA_CK = 512
DSA_SUB = 128
MOE_ROWS = 256
FOX_T = 512
MOBA_FAR = 4


def _cparams(sem):
    return pltpu.CompilerParams(dimension_semantics=sem, vmem_limit_bytes=VMEM_LIMIT)


def _split_cols(h, sizes):
    offs = np.cumsum((0,) + tuple(sizes))
    return [h[..., int(offs[i]):int(offs[i + 1])] for i in range(len(sizes))]


def _rel_bucket(dist):
    n = jnp.maximum(dist, 0)
    nf = jnp.maximum(n, MAX_EXACT).astype(F32)
    log_b = MAX_EXACT + (jnp.log(nf / MAX_EXACT) / math.log(MAX_DISTANCE / MAX_EXACT)
                         * (NUM_BUCKETS - MAX_EXACT)).astype(I32)
    return jnp.where(n < MAX_EXACT, n, jnp.minimum(log_b, NUM_BUCKETS - 1))


def _toeplitz_tiles(bias_by_dist, n_tiles, edge):
    i = jnp.arange(edge)[:, None]
    j = jnp.arange(edge)[None, :]
    d = jnp.arange(n_tiles)[:, None, None] * edge + (i - j)[None]
    d = jnp.clip(d, 0, bias_by_dist.shape[0] - 1)
    return jnp.moveaxis(bias_by_dist[d], -1, 1)


def _layer_norm(v, g, b):
    mu = jnp.mean(v, axis=-1, keepdims=True)
    d = v - mu
    var = jnp.mean(d * d, axis=-1, keepdims=True)
    return d * lax.rsqrt(var + NORM_EPS) * g + b


def _linear_kernel(x_ref, w_ref, o_ref):
    o_ref[...] = jnp.dot(x_ref[...].astype(BF16), w_ref[...], preferred_element_type=F32)


def _linear(x, w, tm=256):
    m, k = x.shape
    n = w.shape[1]
    return pl.pallas_call(
        _linear_kernel,
        grid=(m // tm,),
        in_specs=[pl.BlockSpec((tm, k), lambda i: (i, 0)),
                  pl.BlockSpec((k, n), lambda i: (0, 0))],
        out_specs=pl.BlockSpec((tm, n), lambda i: (i, 0)),
        out_shape=jax.ShapeDtypeStruct((m, n), F32),
        compiler_params=_cparams(("parallel",)),
        name="linear",
    )(x, w)


def _linear_ln_kernel(a_ref, w_ref, x_ref, g_ref, b_ref, o_ref, ob_ref, *, alpha):
    m = jnp.dot(a_ref[...].astype(BF16), w_ref[...], preferred_element_type=F32)
    y = _layer_norm(alpha * x_ref[...] + m, g_ref[...], b_ref[...])
    o_ref[...] = y
    ob_ref[...] = y.astype(BF16)


def _linear_ln(a, w, x, g, b, alpha, tm=256):
    m, k = a.shape
    n = w.shape[1]
    return pl.pallas_call(
        functools.partial(_linear_ln_kernel, alpha=alpha),
        grid=(m // tm,),
        in_specs=[pl.BlockSpec((tm, k), lambda i: (i, 0)),
                  pl.BlockSpec((k, n), lambda i: (0, 0)),
                  pl.BlockSpec((tm, n), lambda i: (i, 0)),
                  pl.BlockSpec((1, n), lambda i: (0, 0)),
                  pl.BlockSpec((1, n), lambda i: (0, 0))],
        out_specs=[pl.BlockSpec((tm, n), lambda i: (i, 0)),
                   pl.BlockSpec((tm, n), lambda i: (i, 0))],
        out_shape=[jax.ShapeDtypeStruct((m, n), F32), jax.ShapeDtypeStruct((m, n), BF16)],
        compiler_params=_cparams(("parallel",)),
        name="linear_ln",
    )(a, w, x, g.reshape(1, n), b.reshape(1, n))


def _float_key(s):
    bits = lax.bitcast_convert_type(s, I32)
    return bits ^ ((bits >> 31) & 0x7FFFFFFF)


def _split_hi_lo(v):
    hi = v.astype(BF16)
    return hi, (v - hi.astype(F32)).astype(BF16)


def _dsa_kernel(qa_ref, qi_ref, wi_ref, kidx_ref, ckv_ref, kvg_ref, wuk_ref, wuv_ref, bias_ref,
                o_ref, key_ref, cn_ref, kcat_ref, m_ref, acc_ref, *, seq, topk, idx_bits):
    qb = pl.program_id(1)
    s0 = qb * DSA_Q
    nc = (s0 + DSA_Q + DSA_CK - 1) // DSA_CK
    n_sub = DSA_CK // LANES
    n_bias = bias_ref.shape[0]

    @pl.when(qb == 0)
    def _():
        ones_col = jnp.where(lax.broadcasted_iota(I32, (DSA_CK, LANES), 1) == 0, 1.0, 0.0).astype(BF16)

        def prep_chunk(c, carry):
            rows = pl.ds(pl.multiple_of(c * DSA_CK, DSA_CK), DSA_CK)
            v = ckv_ref[0, rows, :]
            y = v * lax.rsqrt(jnp.mean(v * v, axis=-1, keepdims=True) + NORM_EPS) * kvg_ref[...]
            cn_ref[rows, :] = jnp.concatenate([y.astype(BF16), ones_col], axis=1)
            k_hi, k_lo = _split_hi_lo(kidx_ref[0, rows, :])
            kcat_ref[rows, :] = jnp.concatenate([k_hi, k_lo, k_hi, k_lo], axis=1)
            return carry
        lax.fori_loop(0, seq // DSA_CK, prep_chunk, 0)

    t_pos = s0 + lax.broadcasted_iota(I32, (DSA_Q, LANES), 0)
    lane = lax.broadcasted_iota(I32, (DSA_Q, LANES), 1)

    q_hi, q_lo = _split_hi_lo(qi_ref[0].reshape(IDX_HEADS * DSA_Q, IDX_DIM))
    qcat = jnp.concatenate([q_hi, q_hi, q_lo, q_lo], axis=1)
    wi = wi_ref[0]

    def score_chunk(c, carry):
        rows = pl.ds(pl.multiple_of(c * DSA_CK, DSA_CK), DSA_CK)
        r = lax.dot_general(qcat, kcat_ref[rows, :], (((1,), (1,)), ((), ())),
                            preferred_element_type=F32)
        r = jnp.maximum(r, 0.0)
        sc = wi[:, 0:1] * r[0:DSA_Q]
        for j in range(1, IDX_HEADS):
            sc = sc + wi[:, j:j + 1] * r[j * DSA_Q:(j + 1) * DSA_Q]
        sc = jnp.where(sc == 0.0, 0.0, sc)
        for u in range(n_sub):
            s_pos = c * DSA_CK + u * LANES + lane
            su = sc[:, u * LANES:(u + 1) * LANES]
            key_ref[c, :, u * LANES:(u + 1) * LANES] = jnp.where(
                s_pos <= t_pos, _float_key(su), KEY_NEG_INF)
        return carry
    lax.fori_loop(0, nc, score_chunk, 0)

    def count(pred_fn):
        def body(c, acc):
            k = key_ref[c]
            for u in range(n_sub):
                acc = acc + pred_fn(k[:, u * LANES:(u + 1) * LANES], c * DSA_CK + u * LANES + lane)
            return acc
        acc = lax.fori_loop(0, nc, body, jnp.zeros((DSA_Q, LANES), F32))
        return jnp.sum(acc, axis=1, keepdims=True)

    kf = float(topk)
    cnt0 = count(lambda k, sp: jnp.where(k >= 0, 1.0, 0.0))
    cand0 = jnp.where(cnt0 >= kf, 0, -2147483648).astype(I32)

    def thr_step(it, cand):
        test = cand | jnp.left_shift(jnp.int32(1), 30 - it)
        cnt = count(lambda k, sp: jnp.where(k >= test, 1.0, 0.0))
        return jnp.where(cnt >= kf, test, cand)
    thr = lax.fori_loop(0, 31, thr_step, cand0)

    n_gt = count(lambda k, sp: jnp.where(k > thr, 1.0, 0.0))
    n_ge = count(lambda k, sp: jnp.where(k >= thr, 1.0, 0.0))
    need = kf - n_gt

    def find_cut():
        def cut_step(it, cut):
            test = cut | jnp.left_shift(jnp.int32(1), idx_bits - 1 - it)
            cnt = count(lambda k, sp: jnp.where(k == thr, jnp.where(sp < test, 1.0, 0.0), 0.0))
            return jnp.where(cnt < need, test, cut)
        return lax.fori_loop(0, idx_bits, cut_step, jnp.zeros((DSA_Q, 1), I32))

    surplus = jnp.where(n_ge > kf, jnp.where(thr != KEY_NEG_INF, 1.0, 0.0), 0.0)
    cut = lax.cond(jnp.max(surplus) > 0.5, find_cut, lambda: jnp.full((DSA_Q, 1), seq, I32))

    ql = jnp.dot(qa_ref[0].astype(BF16), wuk_ref[...], preferred_element_type=F32) * (ATTN_SCALE * LOG2E)
    ql = jnp.concatenate([ql[:, h * A_LATENT:(h + 1) * A_LATENT] for h in range(A_HEADS)],
                         axis=0).astype(BF16)
    m_ref[...] = jnp.full(m_ref.shape, MASKED, F32)
    acc_ref[...] = jnp.zeros(acc_ref.shape, F32)

    def attn_chunk(c, near):
        rows = pl.ds(pl.multiple_of(c * DSA_CK, DSA_CK), DSA_CK)
        cc = cn_ref[rows, :]
        lg = lax.dot_general(ql, cc[:, :A_LATENT], (((1,), (1,)), ((), ())), preferred_element_type=F32)
        k = key_ref[c]
        mask_parts, bias_parts = [], []
        for u in range(n_sub):
            s_pos = c * DSA_CK + u * LANES + lane
            ku = k[:, u * LANES:(u + 1) * LANES]
            take = jnp.where(ku > thr, 0.0, jnp.where(ku == thr, jnp.where(s_pos <= cut, 0.0, MASKED), MASKED))
            mask_parts.append(jnp.where(s_pos <= t_pos, take, MASKED))
            if near:
                bias_parts.append(bias_ref[jnp.clip(qb - (c * n_sub + u), 0, n_bias - 1)])
        lg = lg.reshape(A_HEADS, DSA_Q, DSA_CK) + jnp.concatenate(mask_parts, axis=1)[None]
        if near:
            lg = lg + jnp.concatenate(bias_parts, axis=2)
        lg = lg.reshape(A_HEADS * DSA_Q, DSA_CK)
        m_prev = m_ref[...]
        m_new = jnp.maximum(m_prev, jnp.max(lg, axis=1, keepdims=True))
        a = jnp.exp2(m_prev - m_new)
        p = jnp.exp2(lg - jnp.concatenate([m_new] * n_sub, axis=1))
        acc_ref[...] = (jnp.concatenate([a, a], axis=1) * acc_ref[...]
                        + jnp.dot(p.astype(BF16), cc, preferred_element_type=F32))
        m_ref[...] = m_new

    n_far = jnp.clip((qb - (n_bias - 2)) // n_sub, 0, nc) if n_bias * DSA_SUB > MAX_DISTANCE else 0

    def far_chunk(c, carry):
        attn_chunk(c, False)
        return carry

    def near_chunk(c, carry):
        attn_chunk(c, True)
        return carry
    lax.fori_loop(0, n_far, far_chunk, 0)
    lax.fori_loop(n_far, nc, near_chunk, 0)

    acc = acc_ref[...]
    o_lat = (acc[:, :A_LATENT] / acc[:, A_LATENT:A_LATENT + 1]).astype(BF16)
    out = jnp.dot(o_lat[0:DSA_Q], wuv_ref[0], preferred_element_type=F32)
    for h in range(1, A_HEADS):
        out = out + jnp.dot(o_lat[h * DSA_Q:(h + 1) * DSA_Q], wuv_ref[h], preferred_element_type=F32)
    o_ref[0] = out


def _dsa(qa, qi4, wi, kidx, ckv, kv_g, w_uk, w_uv, bias_tiles):
    bsz, seq, _ = qa.shape
    topk = min(IDX_TOPK_MAX, seq // 4)
    idx_bits = max(1, int(math.ceil(math.log2(seq))))
    hd = A_HEADS * HEAD_DIM
    wuk_bd = jnp.zeros((hd, A_HEADS * A_LATENT), F32)
    wuv_ex = jnp.zeros((A_HEADS, A_LATENT, hd), F32)
    for h in range(A_HEADS):
        wuk_bd = wuk_bd.at[h * HEAD_DIM:(h + 1) * HEAD_DIM, h * A_LATENT:(h + 1) * A_LATENT].set(w_uk[h])
        wuv_ex = wuv_ex.at[h, :, h * HEAD_DIM:(h + 1) * HEAD_DIM].set(w_uv[h])
    n_bias = bias_tiles.shape[0]
    kern = functools.partial(_dsa_kernel, seq=seq, topk=topk, idx_bits=idx_bits)
    return pl.pallas_call(
        kern,
        grid=(bsz, seq // DSA_Q),
        in_specs=[
            pl.BlockSpec((1, DSA_Q, hd), lambda b, q: (b, q, 0)),
            pl.BlockSpec((1, IDX_HEADS, DSA_Q, IDX_DIM), lambda b, q: (b, 0, q, 0)),
            pl.BlockSpec((1, DSA_Q, IDX_HEADS), lambda b, q: (b, q, 0)),
            pl.BlockSpec((1, seq, IDX_DIM), lambda b, q: (b, 0, 0)),
            pl.BlockSpec((1, seq, A_LATENT), lambda b, q: (b, 0, 0)),
            pl.BlockSpec((1, A_LATENT), lambda b, q: (0, 0)),
            pl.BlockSpec((hd, A_HEADS * A_LATENT), lambda b, q: (0, 0)),
            pl.BlockSpec((A_HEADS, A_LATENT, hd), lambda b, q: (0, 0, 0)),
            pl.BlockSpec((n_bias, A_HEADS, DSA_SUB, DSA_SUB), lambda b, q: (0, 0, 0, 0)),
        ],
        out_specs=pl.BlockSpec((1, DSA_Q, hd), lambda b, q: (b, q, 0)),
        out_shape=jax.ShapeDtypeStruct((bsz, seq, hd), F32),
        scratch_shapes=[
            pltpu.VMEM((seq // DSA_CK, DSA_Q, DSA_CK), I32),
            pltpu.VMEM((seq, 2 * A_LATENT), BF16),
            pltpu.VMEM((seq, 4 * IDX_DIM), BF16),
            pltpu.VMEM((A_HEADS * DSA_Q, LANES), F32),
            pltpu.VMEM((A_HEADS * DSA_Q, 2 * A_LATENT), F32),
        ],
        compiler_params=_cparams(("arbitrary", "arbitrary")),
        name="dsa",
    )(qa, qi4, wi, kidx, ckv, kv_g.reshape(1, A_LATENT), wuk_bd.astype(BF16), wuv_ex.astype(BF16),
      bias_tiles)


def _kbar_kernel(k_ref, o_ref):
    o_ref[0] = jnp.mean(k_ref[...], axis=0, keepdims=True)


def _kbar(kb, bsz, seq):
    nb = seq // MOBA_BLOCK
    hd = kb.shape[1]
    return pl.pallas_call(
        _kbar_kernel,
        grid=(bsz * nb,),
        in_specs=[pl.BlockSpec((MOBA_BLOCK, hd), lambda i: (i, 0))],
        out_specs=pl.BlockSpec((1, 1, hd), lambda i: (i, 0, 0)),
        out_shape=jax.ShapeDtypeStruct((bsz * nb, 1, hd), F32),
        compiler_params=_cparams(("parallel",)),
        name="kbar",
    )(kb)


def _moba_kernel(q_ref, k_ref, v_ref, kbar_ref, strip_ref, o_ref, m_ref, acc_ref, *, nb, n_near):
    qi = pl.program_id(2)
    nh = q_ref.shape[1]
    tq = MOBA_BLOCK
    lane = lax.broadcasted_iota(I32, (tq, LANES), 1)
    lanef = lane.astype(F32)
    blk = lane - HEAD_DIM
    past = jnp.where(blk >= 0, jnp.where(blk < qi, 1.0, 0.0), 0.0)

    qs = []
    for h in range(nh):
        q = q_ref[0, h]
        gate = lax.dot_general(q[:, :HEAD_DIM], kbar_ref[0, h], (((1,), (1,)), ((), ())),
                               preferred_element_type=F32, precision=lax.Precision.HIGHEST)
        g = jnp.where(past > 0.5, gate, -jnp.inf)
        keep = jnp.where(blk == qi, 1.0, 0.0)
        for _ in range(MOBA_TOPK):
            mx = jnp.max(g, axis=1, keepdims=True)
            ix = jnp.min(jnp.where(g == mx, lanef, float(LANES)), axis=1, keepdims=True)
            hit = jnp.where(lanef == ix, jnp.where(mx > -jnp.inf, 1.0, 0.0), 0.0)
            keep = jnp.maximum(keep, hit)
            g = jnp.where(hit > 0.5, -jnp.inf, g)
        mask = jnp.where(blk >= 0, jnp.where(blk < nb, jnp.where(keep > 0.5, 0.0, MASKED), 0.0), 0.0)
        qs.append((q * (ATTN_SCALE * LOG2E) + mask).astype(BF16))

    m_ref[...] = jnp.full(m_ref.shape, MASKED, F32)
    acc_ref[...] = jnp.zeros(acc_ref.shape, F32)

    def update(h, s, rows):
        m_prev = m_ref[h]
        m_new = jnp.maximum(m_prev, jnp.max(s, axis=1, keepdims=True))
        a = jnp.exp2(m_prev - m_new)
        p = jnp.exp2(s - jnp.concatenate([m_new] * (s.shape[1] // LANES), axis=1))
        acc_ref[h] = a * acc_ref[h] + jnp.dot(p.astype(BF16), v_ref[0, h, rows, :],
                                              preferred_element_type=F32)
        m_ref[h] = m_new

    def logits(h, rows):
        return lax.dot_general(qs[h], k_ref[0, h, rows, :], (((1,), (1,)), ((), ())),
                               preferred_element_type=F32)

    first = jnp.maximum(qi - (n_near - 1), 0)
    n_far = first // MOBA_FAR

    def far_chunk(i, carry):
        rows = pl.ds(pl.multiple_of(i * (MOBA_FAR * tq), MOBA_FAR * tq), MOBA_FAR * tq)
        for h in range(nh):
            update(h, logits(h, rows), rows)
        return carry
    lax.fori_loop(0, n_far, far_chunk, 0)

    def far_block(n, carry):
        rows = pl.ds(pl.multiple_of(n * tq, tq), tq)
        for h in range(nh):
            update(h, logits(h, rows), rows)
        return carry
    lax.fori_loop(n_far * MOBA_FAR, first, far_block, 0)

    rows = pl.ds(pl.multiple_of(first * tq, tq), n_near * tq)
    outs = []
    for h in range(nh):
        update(h, logits(h, rows) + strip_ref[h, 0], rows)
        acc = acc_ref[h]
        outs.append(acc[:, :HEAD_DIM] / acc[:, HEAD_DIM:HEAD_DIM + 1])
    o_ref[0] = jnp.concatenate(outs, axis=1)


def _moba_strips(tiles):
    nh, n, e, _ = tiles.shape
    causal = jnp.arange(e)[None, :] <= jnp.arange(e)[:, None]
    variants = []
    for v in range(n):
        blocks = []
        for j in range(n):
            if j < v:
                blocks.append(tiles[:, v - j])
            elif j == v:
                blocks.append(jnp.where(causal[None], tiles[:, 0], MASKED))
            else:
                blocks.append(jnp.full((nh, e, e), MASKED, F32))
        variants.append(jnp.concatenate(blocks, axis=-1))
    return jnp.stack(variants, axis=1)


def _moba(q_pad, k_aug, v_aug, kbar_pad, bias_tiles):
    bsz, nh, seq, _ = q_pad.shape
    nb = seq // MOBA_BLOCK
    assert HEAD_DIM + nb <= LANES
    n_near = bias_tiles.shape[1]
    assert n_near == nb or n_near * MOBA_BLOCK - (MOBA_BLOCK - 1) >= MAX_DISTANCE
    strips = _moba_strips(bias_tiles)
    hp = LANES // HEAD_DIM
    return pl.pallas_call(
        functools.partial(_moba_kernel, nb=nb, n_near=n_near),
        grid=(bsz, nh // hp, nb),
        in_specs=[
            pl.BlockSpec((1, hp, MOBA_BLOCK, LANES), lambda b, h, i: (b, h, i, 0)),
            pl.BlockSpec((1, hp, seq, LANES), lambda b, h, i: (b, h, 0, 0)),
            pl.BlockSpec((1, hp, seq, LANES), lambda b, h, i: (b, h, 0, 0)),
            pl.BlockSpec((1, hp, LANES, HEAD_DIM), lambda b, h, i: (b, h, 0, 0)),
            pl.BlockSpec((hp, 1, MOBA_BLOCK, n_near * MOBA_BLOCK),
                         lambda b, h, i: (h, jnp.minimum(i, n_near - 1), 0, 0)),
        ],
        out_specs=pl.BlockSpec((1, MOBA_BLOCK, LANES), lambda b, h, i: (b, i, h)),
        out_shape=jax.ShapeDtypeStruct((bsz, seq, nh * HEAD_DIM), F32),
        scratch_shapes=[
            pltpu.VMEM((hp, MOBA_BLOCK, LANES), F32),
            pltpu.VMEM((hp, MOBA_BLOCK, LANES), F32),
        ],
        compiler_params=_cparams(("parallel", "parallel", "arbitrary")),
        name="moba",
    )(q_pad, k_aug, v_aug, kbar_pad, strips)


def _cumsum_kernel(f_ref, b_ref, o_ref, *, seq):
    z = f_ref[...] + b_ref[...]
    y = jnp.minimum(z, 0.0) - jnp.log1p(jnp.exp(-jnp.abs(z)))
    lane = lax.broadcasted_iota(I32, y.shape, 1)
    shift = 1
    while shift < seq:
        y = y + jnp.where(lane >= shift, pltpu.roll(y, shift, axis=1), 0.0)
        shift *= 2
    y2 = y * LOG2E
    hi = y2.astype(BF16).astype(F32)
    r1 = y2 - hi
    mid = r1.astype(BF16).astype(F32)
    o_ref[0] = hi
    o_ref[1] = mid
    o_ref[2] = (r1 - mid).astype(BF16).astype(F32)


def _cumsum_logsig(fz_t, bias_rows):
    rows, seq = fz_t.shape
    return pl.pallas_call(
        functools.partial(_cumsum_kernel, seq=seq),
        grid=(1,),
        in_specs=[pl.BlockSpec((rows, seq), lambda i: (0, 0)),
                  pl.BlockSpec((rows, 1), lambda i: (0, 0))],
        out_specs=pl.BlockSpec((3, rows, seq), lambda i: (0, 0, 0)),
        out_shape=jax.ShapeDtypeStruct((3, rows, seq), F32),
        compiler_params=_cparams(("arbitrary",)),
        name="cumsum_logsig",
    )(fz_t, bias_rows)


def _fox_kernel(q_ref, k_ref, v_ref, o_ref, m_ref, acc_ref):
    qi = pl.program_id(2)
    t = FOX_T
    nh = q_ref.shape[1]
    lane = lax.broadcasted_iota(I32, (1, LANES), 1)
    qscale = jnp.where(lane < HEAD_DIM, ATTN_SCALE * LOG2E, 1.0)
    qs = [(q_ref[0, h] * qscale).astype(BF16) for h in range(nh)]
    m_ref[...] = jnp.full(m_ref.shape, MASKED, F32)
    acc_ref[...] = jnp.zeros(acc_ref.shape, F32)

    def update(h, n, diagonal):
        rows = pl.ds(pl.multiple_of(n * t, t), t)
        s = lax.dot_general(qs[h], k_ref[0, h, rows, :], (((1,), (1,)), ((), ())),
                            preferred_element_type=F32)
        if diagonal:
            ri = lax.broadcasted_iota(I32, (t, t), 0)
            ci = lax.broadcasted_iota(I32, (t, t), 1)
            s = jnp.where(ci <= ri, s, MASKED)
        m_prev = m_ref[h]
        m_new = jnp.maximum(m_prev, jnp.max(s, axis=1, keepdims=True))
        a = jnp.exp2(m_prev - m_new)
        p = jnp.exp2(s - jnp.concatenate([m_new] * (t // LANES), axis=1))
        acc_ref[h] = a * acc_ref[h] + jnp.dot(p.astype(BF16), v_ref[0, h, rows, :],
                                              preferred_element_type=F32)
        m_ref[h] = m_new

    def past_block(n, carry):
        for h in range(nh):
            update(h, n, False)
        return carry
    lax.fori_loop(0, qi, past_block, 0)
    outs = []
    for h in range(nh):
        update(h, qi, True)
        acc = acc_ref[h]
        outs.append(acc[:, :HEAD_DIM] / acc[:, HEAD_DIM:HEAD_DIM + 1])
    o_ref[0] = jnp.concatenate(outs, axis=1)


def _fox(q_aug, k_aug, v_aug):
    bsz, nh, seq, _ = q_aug.shape
    t = FOX_T
    hp = LANES // HEAD_DIM
    return pl.pallas_call(
        _fox_kernel,
        grid=(bsz, nh // hp, seq // t),
        in_specs=[
            pl.BlockSpec((1, hp, t, LANES), lambda b, h, i: (b, h, i, 0)),
            pl.BlockSpec((1, hp, seq, LANES), lambda b, h, i: (b, h, 0, 0)),
            pl.BlockSpec((1, hp, seq, LANES), lambda b, h, i: (b, h, 0, 0)),
        ],
        out_specs=pl.BlockSpec((1, t, LANES), lambda b, h, i: (b, i, h)),
        out_shape=jax.ShapeDtypeStruct((bsz, seq, nh * HEAD_DIM), F32),
        scratch_shapes=[
            pltpu.VMEM((hp, t, LANES), F32),
            pltpu.VMEM((hp, t, LANES), F32),
        ],
        compiler_params=_cparams(("parallel", "parallel", "arbitrary")),
        name="fox",
    )(q_aug, k_aug, v_aug)


def _router_kernel(x_ref, w_ref, b_ref, e_ref, g_ref, r_ref, cnt_ref):
    step = pl.program_id(0)
    lg = jnp.dot(x_ref[...], w_ref[...], preferred_element_type=F32,
                 precision=lax.Precision.HIGHEST) + b_ref[...]
    tm, ne = lg.shape
    col = lax.broadcasted_iota(I32, (tm, ne), 1).astype(F32)
    vals, idxs = [], []
    for _ in range(TOP_K):
        mx = jnp.max(lg, axis=1, keepdims=True)
        ix = jnp.min(jnp.where(lg == mx, col, float(ne)), axis=1, keepdims=True)
        vals.append(mx)
        idxs.append(ix)
        lg = jnp.where(col == ix, -jnp.inf, lg)
    ex = [jnp.exp(v - vals[0]) for v in vals]
    den = ex[0]
    for e in ex[1:]:
        den = den + e

    @pl.when(step == 0)
    def _():
        cnt_ref[...] = jnp.zeros(cnt_ref.shape, F32)

    picked = jnp.where(col == idxs[0], 1.0, 0.0)
    for ix in idxs[1:]:
        picked = picked + jnp.where(col == ix, 1.0, 0.0)
    earlier = (lax.broadcasted_iota(I32, (tm, tm), 1) < lax.broadcasted_iota(I32, (tm, tm), 0))
    before = cnt_ref[...] + jnp.dot(jnp.where(earlier, 1.0, 0.0).astype(BF16), picked.astype(BF16),
                                    preferred_element_type=F32)
    cnt_ref[...] = cnt_ref[...] + jnp.sum(picked, axis=0, keepdims=True)

    kcol = lax.broadcasted_iota(I32, (tm, TOP_K), 1)
    e_out = jnp.zeros((tm, TOP_K), I32)
    g_out = jnp.zeros((tm, TOP_K), F32)
    r_out = jnp.zeros((tm, TOP_K), I32)
    for k in range(TOP_K):
        rank = jnp.sum(jnp.where(col == idxs[k], before, 0.0), axis=1, keepdims=True)
        e_out = jnp.where(kcol == k, idxs[k].astype(I32), e_out)
        g_out = jnp.where(kcol == k, ex[k] / den, g_out)
        r_out = jnp.where(kcol == k, rank.astype(I32), r_out)
    e_ref[...] = e_out
    g_ref[...] = g_out
    r_ref[...] = r_out


def _router(x, w, b, tm=512):
    n, d = x.shape
    ne = w.shape[1]
    return pl.pallas_call(
        _router_kernel,
        grid=(n // tm,),
        in_specs=[pl.BlockSpec((tm, d), lambda i: (i, 0)),
                  pl.BlockSpec((d, ne), lambda i: (0, 0)),
                  pl.BlockSpec((1, ne), lambda i: (0, 0))],
        out_specs=[pl.BlockSpec((tm, TOP_K), lambda i: (i, 0)),
                   pl.BlockSpec((tm, TOP_K), lambda i: (i, 0)),
                   pl.BlockSpec((tm, TOP_K), lambda i: (i, 0)),
                   pl.BlockSpec((1, ne), lambda i: (0, 0))],
        out_shape=[jax.ShapeDtypeStruct((n, TOP_K), I32), jax.ShapeDtypeStruct((n, TOP_K), F32),
                   jax.ShapeDtypeStruct((n, TOP_K), I32), jax.ShapeDtypeStruct((1, ne), F32)],
        compiler_params=_cparams(("arbitrary",)),
        name="router",
    )(x, w, b.reshape(1, ne))


def _expert_kernel(be_ref, nu_ref, xs_ref, wgu_ref, bgu_ref, wd_ref, bd_ref, o_ref, wgu_bf, wd_bf):
    i = pl.program_id(0)
    d_ff = wd_ref.shape[1]

    @pl.when(jnp.logical_or(i == 0, be_ref[i] != be_ref[jnp.maximum(i - 1, 0)]))
    def _():
        wgu_bf[...] = wgu_ref[0].astype(BF16)
        wd_bf[...] = wd_ref[0].astype(BF16)

    @pl.when(i < nu_ref[0])
    def _():
        h = jnp.dot(xs_ref[...], wgu_bf[...], preferred_element_type=F32) + bgu_ref[0]
        g = jnp.minimum(h[:, :d_ff], SWIGLU_LIMIT)
        u = jnp.clip(h[:, d_ff:], -SWIGLU_LIMIT, SWIGLU_LIMIT)
        act = (u + 1.0) * g * (1.0 / (1.0 + jnp.exp(-SWIGLU_ALPHA * g)))
        o_ref[...] = jnp.dot(act.astype(BF16), wd_bf[...], preferred_element_type=F32) + bd_ref[0]

    @pl.when(i >= nu_ref[0])
    def _():
        o_ref[...] = jnp.zeros(o_ref.shape, o_ref.dtype)


def _experts(xs, block_e, n_used, w_gu, b_gu, w_down, b_down):
    n_rows, d = xs.shape
    ne, _, f2 = w_gu.shape
    d_ff = f2 // 2
    n_blocks = n_rows // MOE_ROWS
    grid_spec = pltpu.PrefetchScalarGridSpec(
        num_scalar_prefetch=2,
        grid=(n_blocks,),
        in_specs=[
            pl.BlockSpec((MOE_ROWS, d), lambda i, be, nu: (i, 0)),
            pl.BlockSpec((1, d, f2), lambda i, be, nu: (be[i], 0, 0)),
            pl.BlockSpec((1, 1, f2), lambda i, be, nu: (be[i], 0, 0)),
            pl.BlockSpec((1, d_ff, d), lambda i, be, nu: (be[i], 0, 0)),
            pl.BlockSpec((1, 1, d), lambda i, be, nu: (be[i], 0, 0)),
        ],
        out_specs=pl.BlockSpec((MOE_ROWS, d), lambda i, be, nu: (i, 0)),
        scratch_shapes=[pltpu.VMEM((d, f2), BF16), pltpu.VMEM((d_ff, d), BF16)],
    )
    return pl.pallas_call(
        _expert_kernel,
        grid_spec=grid_spec,
        out_shape=jax.ShapeDtypeStruct((n_rows, d), F32),
        compiler_params=_cparams(("arbitrary",)),
        name="experts",
    )(block_e, n_used, xs, w_gu, b_gu.reshape(ne, 1, f2), w_down, b_down.reshape(ne, 1, d))


def _combine_ln_kernel(y_ref, gt_ref, x_ref, g_ref, b_ref, o_ref, *, alpha):
    gt = gt_ref[...]
    y = y_ref[0] * gt[:, 0:1]
    for k in range(1, TOP_K):
        y = y + y_ref[k] * gt[:, k:k + 1]
    o_ref[...] = _layer_norm(alpha * x_ref[...] + y, g_ref[...], b_ref[...])


def _combine_ln(yg, gates, x, g, b, alpha, tm=256):
    n, d = x.shape
    return pl.pallas_call(
        functools.partial(_combine_ln_kernel, alpha=alpha),
        grid=(n // tm,),
        in_specs=[pl.BlockSpec((TOP_K, tm, d), lambda i: (0, i, 0)),
                  pl.BlockSpec((tm, TOP_K), lambda i: (i, 0)),
                  pl.BlockSpec((tm, d), lambda i: (i, 0)),
                  pl.BlockSpec((1, d), lambda i: (0, 0)),
                  pl.BlockSpec((1, d), lambda i: (0, 0))],
        out_specs=pl.BlockSpec((tm, d), lambda i: (i, 0)),
        out_shape=jax.ShapeDtypeStruct((n, d), F32),
        compiler_params=_cparams(("parallel",)),
        name="combine_ln",
    )(yg, gates, x, g.reshape(1, d), b.reshape(1, d))


def _moe(x, xb, router_w, router_b, w_gu, b_gu, w_down, b_down, ln_g, ln_b, alpha):
    n, d = x.shape
    top_e, gates, rank, counts = _router(x, router_w, router_b)
    nk = n * TOP_K
    counts = counts[0].astype(I32)
    padded = (counts + MOE_ROWS - 1) // MOE_ROWS * MOE_ROWS
    start = jnp.cumsum(counts) - counts
    pend = jnp.cumsum(padded)
    pstart = pend - padded
    n_blocks = nk // MOE_ROWS + N_EXPERTS
    n_rows = n_blocks * MOE_ROWS
    block_e = jnp.minimum(jnp.searchsorted(pend, jnp.arange(n_blocks) * MOE_ROWS, side='right'),
                          N_EXPERTS - 1).astype(I32)
    n_used = (pend[-1] // MOE_ROWS).astype(I32).reshape(1)
    pos = pstart[top_e] + rank
    order = jnp.argsort(top_e.reshape(-1))
    row_e = jnp.repeat(block_e, MOE_ROWS)
    j = jnp.arange(n_rows) - pstart[row_e]
    row_tok = jnp.where(j < counts[row_e], order[jnp.clip(start[row_e] + j, 0, nk - 1)] // TOP_K, 0)
    xs = jnp.take(xb, row_tok.astype(I32), axis=0)
    y_rows = _experts(xs, block_e, n_used, w_gu, b_gu, w_down, b_down)
    yg = jnp.take(y_rows, pos.T.reshape(-1), axis=0).reshape(TOP_K, n, d)
    return _combine_ln(yg, gates, x, ln_g, ln_b, alpha)


def _pad_cols(w, mult=LANES):
    n = w.shape[1]
    pad = (-n) % mult
    return jnp.pad(w, ((0, 0), (0, pad))) if pad else w


def _heads(t, bsz, seq, nh):
    return t.reshape(bsz, seq, nh, HEAD_DIM).transpose(0, 2, 1, 3)


def _even_mixer(xf, bsz, seq, w_in, w_uk, w_uv, kv_g, bias_by_dist):
    proj = _linear(xf, _pad_cols(w_in).astype(BF16))
    qa, ckv, qi, ki, wi, qb, kb, vb = _split_cols(proj[:, :sum(EVEN_SPLITS)], EVEN_SPLITS)
    n_sub_tiles = min(-(-(MAX_DISTANCE) // DSA_SUB) + 1, seq // DSA_SUB)
    dsa_bias = _toeplitz_tiles(bias_by_dist[:, :A_HEADS], n_sub_tiles, DSA_SUB)
    qi4 = qi.reshape(bsz, seq, IDX_HEADS, IDX_DIM).transpose(0, 2, 1, 3)
    oa = _dsa(qa.reshape(bsz, seq, -1), qi4, wi.reshape(bsz, seq, IDX_HEADS),
              ki.reshape(bsz, seq, IDX_DIM), ckv.reshape(bsz, seq, A_LATENT), kv_g, w_uk, w_uv, dsa_bias)
    nb = seq // MOBA_BLOCK
    kbar = _kbar(kb, bsz, seq).reshape(bsz, nb, B_HEADS, HEAD_DIM).transpose(0, 2, 1, 3)
    kbar_pad = jnp.pad(kbar, ((0, 0), (0, 0), (HEAD_DIM, LANES - HEAD_DIM - nb), (0, 0)))
    n_moba_tiles = min(-(-(MAX_DISTANCE + MOBA_BLOCK) // MOBA_BLOCK), nb)
    moba_bias = jnp.moveaxis(_toeplitz_tiles(bias_by_dist[:, A_HEADS:], n_moba_tiles, MOBA_BLOCK), 0, 1)
    lead = (bsz, B_HEADS, seq)
    block_1hot = (jnp.arange(seq)[:, None] // MOBA_BLOCK == jnp.arange(nb)[None, :]).astype(F32)
    q_pad = jnp.concatenate([_heads(qb, bsz, seq, B_HEADS), jnp.zeros(lead + (LANES - HEAD_DIM,), F32)], axis=-1)
    k_aug = jnp.concatenate([_heads(kb, bsz, seq, B_HEADS), jnp.broadcast_to(block_1hot, lead + (nb,)),
                             jnp.zeros(lead + (LANES - HEAD_DIM - nb,), F32)], axis=-1).astype(BF16)
    v_aug = jnp.concatenate([_heads(vb, bsz, seq, B_HEADS), jnp.ones(lead + (1,), F32),
                             jnp.zeros(lead + (LANES - HEAD_DIM - 1,), F32)], axis=-1).astype(BF16)
    ob = _moba(q_pad, k_aug, v_aug, kbar_pad, moba_bias)
    return jnp.concatenate([oa.reshape(bsz * seq, -1), ob.reshape(bsz * seq, -1)], axis=-1)


def _odd_mixer(xf, bsz, seq, w_in, b_f):
    proj = _linear(xf, _pad_cols(w_in).astype(BF16))
    q, k, v, fz = _split_cols(proj[:, :sum(ODD_SPLITS)], ODD_SPLITS)
    fz_t = fz.reshape(bsz, seq, C_HEADS).transpose(0, 2, 1).reshape(bsz * C_HEADS, seq)
    b_rows = jnp.tile(b_f, bsz).reshape(bsz * C_HEADS, 1)
    fparts = _cumsum_logsig(fz_t, b_rows).reshape(3, bsz, C_HEADS, seq)
    fparts = jnp.moveaxis(fparts, 0, -1)
    ones = jnp.ones_like(fparts)
    pad = LANES - HEAD_DIM - 6
    zeros = jnp.zeros(fparts.shape[:-1] + (pad,), F32)
    q_aug = jnp.concatenate([_heads(q, bsz, seq, C_HEADS), fparts, ones, zeros], axis=-1)
    k_aug = jnp.concatenate([_heads(k, bsz, seq, C_HEADS), ones, -fparts, zeros], axis=-1).astype(BF16)
    v_aug = jnp.concatenate([_heads(v, bsz, seq, C_HEADS), ones[..., :1],
                             jnp.zeros(fparts.shape[:-1] + (LANES - HEAD_DIM - 1,), F32)], axis=-1).astype(BF16)
    return _fox(q_aug, k_aug, v_aug).reshape(bsz * seq, C_HEADS * HEAD_DIM)


def kernel(x, even_w_in, even_w_uk, even_w_uv, even_kv_g, even_w_out, odd_w_in, odd_b_f, odd_w_out,
           rel_bias, ln_g, ln_b, router_w, router_b, exp_w_gu, exp_b_gu, exp_w_down, exp_b_down):
    bsz, seq, d = x.shape
    depth = ln_g.shape[0]
    alpha = (2 * depth) ** 0.25
    bias_by_dist = (rel_bias[_rel_bucket(jnp.arange(seq))] - rel_bias[NUM_BUCKETS - 1]) * LOG2E
    xf = x.reshape(bsz * seq, d)
    for layer in range(depth):
        j = layer // 2
        if layer % 2 == 0:
            a = _even_mixer(xf, bsz, seq, even_w_in[j], even_w_uk[j], even_w_uv[j], even_kv_g[j], bias_by_dist)
            w_out = even_w_out[j]
        else:
            a = _odd_mixer(xf, bsz, seq, odd_w_in[j], odd_b_f[j])
            w_out = odd_w_out[j]
        xf, xb = _linear_ln(a, w_out.astype(BF16), xf, ln_g[layer, 0], ln_b[layer, 0], alpha)
        xf = _moe(xf, xb, router_w[layer], router_b[layer], exp_w_gu[layer], exp_b_gu[layer],
                  exp_w_down[layer], exp_b_down[layer], ln_g[layer, 1], ln_b[layer, 1], alpha)
    return xf.reshape(bsz, seq, d)
```
